```python
import jax, jax.numpy as jnp
from jax import lax
import numpy as np

D_MODEL = 1024
BATCH = 2
SEQ = 16384
DEPTH = 4

PLE_DIM = 256
HG_WIDTH = 512
HG_HEADS = 4
HG_DK = HG_WIDTH // HG_HEADS
HG_CHUNK = 64
AT_WIDTH = D_MODEL - HG_WIDTH
AT_HEAD_DIM = 64
AT_Q_HEADS = AT_WIDTH // AT_HEAD_DIM
AT_KV_HEADS = 2
AT_GROUP = AT_Q_HEADS // AT_KV_HEADS
WINDOW = 128
D_FF = 2816
CONV_W = 3
EPS = 1e-6
MASK_VALUE = -1e30
LB_FLOOR = 1e-30
SPLIT_SIZES = [HG_WIDTH, HG_WIDTH, HG_WIDTH, HG_WIDTH, AT_WIDTH,
               AT_KV_HEADS * AT_HEAD_DIM, AT_KV_HEADS * AT_HEAD_DIM]
IN_WIDTH = sum(SPLIT_SIZES)

kernel_name = "hymba_style_hgrn2_swa_sink_convffn_ple"


def rmsnorm(x, g):
    xf = x.astype(jnp.float32)
    y = xf * lax.rsqrt(jnp.mean(xf * xf, axis=-1, keepdims=True) + EPS)
    return (y * g.astype(jnp.float32)).astype(x.dtype)


def hgrn2_mixer(q_pre, f_pre, i_in, g_pre, lb, norm_g):
    B, S, _ = q_pre.shape
    nc = S // HG_CHUNK
    f32 = jnp.float32
    lbf = lb.astype(f32)
    z = f_pre.astype(f32)
    q = jax.nn.silu(q_pre.astype(f32))
    logf = jnp.logaddexp(jnp.log(jnp.maximum(lbf, LB_FLOOR)),
                         jnp.log1p(-lbf) + jax.nn.log_sigmoid(z))
    k = (1.0 - lbf) * jax.nn.sigmoid(-z)
    v = i_in.astype(f32)

    def to_chunks(t):
        return t.reshape(B, nc, HG_CHUNK, HG_HEADS, HG_DK).transpose(1, 0, 3, 2, 4)

    qc, kc, vc, lc = to_chunks(q), to_chunks(k), to_chunks(v), to_chunks(logf)
    causal = jnp.tril(jnp.ones((HG_CHUNK, HG_CHUNK), dtype=bool))[:, :, None]

    def step(state, inp):
        qi, ki, vi, li = inp
        b = jnp.cumsum(li, axis=2)
        diff = b[:, :, :, None, :] - b[:, :, None, :, :]
        decay = jnp.where(causal, jnp.exp(jnp.where(causal, diff, 0.0)), 0.0)
        att = jnp.einsum('bhtk,bhsk,bhtsk->bhts', qi, ki, decay)
        o = (jnp.einsum('bhts,bhsv->bhtv', att, vi)
             + jnp.einsum('bhtk,bhkv->bhtv', qi * jnp.exp(b), state))
        b_last = b[:, :, -1:, :]
        new_state = (jnp.exp(b_last[:, :, 0, :])[..., None] * state
                     + jnp.einsum('bhsk,bhsv->bhkv', ki * jnp.exp(b_last - b), vi))
        return new_state, o

    s0 = jnp.zeros((B, HG_HEADS, HG_DK, HG_DK), f32)
    _, o = lax.scan(step, s0, (qc, kc, vc, lc))
    o = o.transpose(1, 0, 3, 2, 4).reshape(B, S, HG_HEADS, HG_DK)
    o = o * lax.rsqrt(jnp.mean(o * o, axis=-1, keepdims=True) + EPS)
    o = o * norm_g.astype(f32).reshape(HG_HEADS, HG_DK)
    o = o.reshape(B, S, HG_WIDTH) * jax.nn.sigmoid(g_pre.astype(f32))
    return o.astype(i_in.dtype)


def swa_sink_attention(q, k, v, sinks):
    B, S, _ = q.shape
    nb = S // WINDOW
    q = q.reshape(B, nb, WINDOW, AT_KV_HEADS, AT_GROUP, AT_HEAD_DIM)
    k = k.reshape(B, nb, WINDOW, AT_KV_HEADS, AT_HEAD_DIM)
    v = v.reshape(B, nb, WINDOW, AT_KV_HEADS, AT_HEAD_DIM)
    pad = ((0, 0), (1, 0), (0, 0), (0, 0), (0, 0))
    kk = jnp.concatenate([jnp.pad(k[:, :-1], pad), k], axis=2)
    vv = jnp.concatenate([jnp.pad(v[:, :-1], pad), v], axis=2)
    scale = AT_HEAD_DIM ** -0.5
    scores = jnp.einsum('bnqhgd,bnkhd->bnhgqk', q, kk).astype(jnp.float32) * scale
    qpos = jnp.arange(WINDOW)[:, None] + WINDOW
    kpos = jnp.arange(2 * WINDOW)[None, :]
    rel = qpos - kpos
    band = (rel >= 0) & (rel < WINDOW)
    first = (jnp.arange(nb) == 0)[:, None, None]
    valid = band[None] & ~(first & (kpos < WINDOW)[None])
    scores = jnp.where(valid[None, :, None, None], scores, MASK_VALUE)
    sink = sinks.astype(jnp.float32).reshape(AT_KV_HEADS, AT_GROUP)[None, None, :, :, None, None]
    m = jnp.maximum(jnp.max(scores, axis=-1, keepdims=True), sink)
    e = jnp.exp(scores - m)
    probs = (e / (jnp.sum(e, axis=-1, keepdims=True) + jnp.exp(sink - m))).astype(v.dtype)
    out = jnp.einsum('bnhgqk,bnkhd->bnqhgd', probs, vv)
    return out.reshape(B, S, AT_WIDTH)


def conv_ffn(u, w_up, conv_w, conv_b, w_down):
    h = u @ w_up
    a, b = h[..., :D_FF], h[..., D_FF:]
    a = lax.conv_general_dilated(
        a, conv_w[:, None, :], window_strides=(1,), padding=[(CONV_W - 1, 0)],
        dimension_numbers=('NWC', 'WIO', 'NWC'), feature_group_count=D_FF) + conv_b
    return (jax.nn.silu(a) * b) @ w_down


def setup_inputs(seed: int = 0) -> dict:
    key = jax.random.key(seed)
    ks = jax.random.split(key, 20)
    f32 = jnp.float32
    nrm = lambda k, shape, s: jax.random.normal(k, shape, f32) * s
    res_scale = (2.0 * DEPTH) ** -0.5
    return {
        "x": nrm(ks[0], (BATCH, SEQ, D_MODEL), 1.0),
        "p": nrm(ks[1], (DEPTH, BATCH, SEQ, PLE_DIM), 1.0),
        "g_mix": 1.0 + nrm(ks[2], (DEPTH, D_MODEL), 0.02),
        "w_in": nrm(ks[3], (DEPTH, D_MODEL, IN_WIDTH), D_MODEL ** -0.5),
        "lb_logits": nrm(ks[4], (DEPTH, HG_WIDTH), 0.1),
        "hg_norm_g": 1.0 + nrm(ks[5], (DEPTH, HG_WIDTH), 0.02),
        "attn_sinks": nrm(ks[6], (DEPTH, AT_Q_HEADS), 0.5),
        "w_out": nrm(ks[7], (DEPTH, D_MODEL, D_MODEL), D_MODEL ** -0.5 * res_scale),
        "g_ffn": 1.0 + nrm(ks[8], (DEPTH, D_MODEL), 0.02),
        "w_up": nrm(ks[9], (DEPTH, D_MODEL, 2 * D_FF), D_MODEL ** -0.5),
        "conv_w": nrm(ks[10], (DEPTH, CONV_W, D_FF), CONV_W ** -0.5),
        "conv_b": nrm(ks[11], (DEPTH, D_FF), 0.01),
        "w_down": nrm(ks[12], (DEPTH, D_FF, D_MODEL), D_FF ** -0.5 * res_scale),
        "g_ple": 1.0 + nrm(ks[13], (DEPTH, D_MODEL), 0.02),
        "w_ple_gate": nrm(ks[14], (DEPTH, D_MODEL, D_MODEL), D_MODEL ** -0.5),
        "w_ple_up": nrm(ks[15], (DEPTH, PLE_DIM, D_MODEL), PLE_DIM ** -0.5 * res_scale),
        "g_final": 1.0 + nrm(ks[16], (D_MODEL,), 0.02),
    }


def reference(x, p, g_mix, w_in, lb_logits, hg_norm_g, attn_sinks, w_out, g_ffn,
              w_up, conv_w, conv_b, w_down, g_ple, w_ple_gate, w_ple_up, g_final):
    lb_p = jax.nn.softmax(lb_logits.astype(jnp.float32), axis=0)
    lb_all = jnp.cumsum(lb_p, axis=0) - lb_p[0]
    cuts = list(np.cumsum(SPLIT_SIZES)[:-1])
    h = x
    for i in range(DEPTH):
        u = rmsnorm(h, g_mix[i])
        proj = u @ w_in[i]
        hq, hf, hi, hg, aq, ak, av = jnp.split(proj, cuts, axis=-1)
        y_hg = hgrn2_mixer(hq, hf, hi, hg, lb_all[i], hg_norm_g[i])
        y_at = swa_sink_attention(aq, ak, av, attn_sinks[i])
        h = h + jnp.concatenate([y_hg, y_at], axis=-1) @ w_out[i]
        h = h + conv_ffn(rmsnorm(h, g_ffn[i]), w_up[i], conv_w[i], conv_b[i], w_down[i])
        gate = jax.nn.sigmoid(rmsnorm(h, g_ple[i]) @ w_ple_gate[i])
        h = h + gate * (p[i] @ w_ple_up[i])
    return rmsnorm(h, g_final)
```

```python
import functools

import jax
import jax.numpy as jnp
from jax import lax
from jax.experimental import pallas as pl
from jax.experimental.pallas import tpu as pltpu

F32 = jnp.float32
BF16 = jnp.bfloat16

D_MODEL = 1024
PLE_DIM = 256
HG_WIDTH = 512
HG_HEADS = 4
HG_DK = 128
HG_CHUNK = 64
AT_WIDTH = 512
AT_HEAD_DIM = 64
AT_Q_HEADS = 8
AT_KV_HEADS = 2
AT_GROUP = 4
WINDOW = 128
KV_WIDTH = AT_KV_HEADS * AT_HEAD_DIM
D_FF = 2816
EPS = 1e-6
MASK_VALUE = -1e30
LB_FLOOR = 1e-30
IN_WIDTH = 4 * HG_WIDTH + AT_WIDTH + 2 * KV_WIDTH

C_Q, C_K, C_V, C_G, C_AQ, C_AK, C_AV = 0, 512, 1024, 1536, 2048, 2560, 2688

FF_CHUNK = 256
N_FF_CHUNKS = D_FF // FF_CHUNK
CONV_PAD = 8

VMEM_LIMIT_BYTES = 56 * 1024 * 1024

TOKEN_TILE = 512


def _dot(a, b):
    return jnp.dot(a, b, preferred_element_type=F32)


def _dot_nt(a, b):
    return lax.dot_general(a, b, (((1,), (1,)), ((), ())), preferred_element_type=F32)


def _dot_tn(a, b):
    return lax.dot_general(a, b, (((0,), (0,)), ((), ())), preferred_element_type=F32)


def _sigmoid(x):
    return 1.0 / (1.0 + jnp.exp(-x))


def _rms_scale(x):
    return x * lax.rsqrt(jnp.mean(x * x, axis=-1, keepdims=True) + EPS)


def _inproj_kernel(h_ref, g_ref, w_ref, lbc_ref, pb_ref, lf_ref):
    u = (_rms_scale(h_ref[...]) * g_ref[...]).astype(BF16)

    def proj(lo, hi):
        return _dot(u, w_ref[:, lo:hi])

    hq = proj(0, 512)
    pb_ref[:, C_Q:C_Q + 512] = (hq * _sigmoid(hq)).astype(BF16)

    z = proj(512, 1024)
    log_lb = lbc_ref[0:1, :]
    log_1mlb = lbc_ref[1:2, :]
    one_mlb = lbc_ref[2:3, :]
    e = jnp.exp(-jnp.abs(z))
    log_sig = jnp.minimum(z, 0.0) - jnp.log1p(e)
    c = log_1mlb + log_sig
    lf_ref[...] = jnp.maximum(log_lb, c) + jnp.log1p(jnp.exp(-jnp.abs(log_lb - c)))
    sig_neg = jnp.where(z >= 0.0, e, 1.0) / (1.0 + e)
    pb_ref[:, C_K:C_K + 512] = (one_mlb * sig_neg).astype(BF16)

    pb_ref[:, C_V:C_V + 512] = proj(1024, 1536).astype(BF16)
    pb_ref[:, C_G:C_G + 512] = _sigmoid(proj(1536, 2048)).astype(BF16)
    pb_ref[:, C_AQ:C_AQ + 512] = (proj(2048, 2560) * (AT_HEAD_DIM ** -0.5)).astype(BF16)
    pb_ref[:, C_AK:C_AK + 256] = proj(2560, 2816).astype(BF16)


def _inproj(h, g, w, lbc, ts):
    B, S, _ = h.shape
    const = lambda b, s: (0, 0)
    return pl.pallas_call(
        _inproj_kernel,
        grid=(B, S // ts),
        in_specs=[
            pl.BlockSpec((None, ts, D_MODEL), lambda b, s: (b, s, 0)),
            pl.BlockSpec((1, D_MODEL), const),
            pl.BlockSpec((D_MODEL, IN_WIDTH), const, pipeline_mode=pl.Buffered(1)),
            pl.BlockSpec((3, HG_WIDTH), const),
        ],
        out_specs=[
            pl.BlockSpec((None, ts, IN_WIDTH), lambda b, s: (b, s, 0)),
            pl.BlockSpec((None, ts, HG_WIDTH), lambda b, s: (b, s, 0)),
        ],
        out_shape=[
            jax.ShapeDtypeStruct((B, S, IN_WIDTH), BF16),
            jax.ShapeDtypeStruct((B, S, HG_WIDTH), F32),
        ],
        compiler_params=pltpu.CompilerParams(
            dimension_semantics=("arbitrary", "arbitrary"), vmem_limit_bytes=VMEM_LIMIT_BYTES),
        name="inproj",
    )(h, g, w, lbc)


def _hgrn_chunk(c, pb_ref, lf_ref, ng_ref, st_ref, y_ref):
    C = HG_CHUNK
    rows = pl.ds(pl.multiple_of(c * C, C), C)
    lf = lf_ref[rows, :]
    q = pb_ref[rows, C_Q:C_Q + 512].astype(F32)
    k = pb_ref[rows, C_K:C_K + 512].astype(F32)
    v = pb_ref[rows, C_V:C_V + 512]
    gs = pb_ref[rows, C_G:C_G + 512].astype(F32)

    tt = lax.broadcasted_iota(jnp.int32, (C, C), 0)
    ss = lax.broadcasted_iota(jnp.int32, (C, C), 1)
    row = lax.broadcasted_iota(jnp.int32, (C, HG_WIDTH), 0)

    tril = jnp.where(tt >= ss, 1.0, 0.0).astype(BF16)
    hi = lf.astype(BF16)
    r1 = lf - hi.astype(F32)
    mid = r1.astype(BF16)
    lo = (r1 - mid.astype(F32)).astype(BF16)
    b3 = _dot(tril, jnp.concatenate([hi, mid, lo], axis=1))
    b = b3[:, 0:512] + b3[:, 512:1024] + b3[:, 1024:1536]
    b_last = b[C - 1:C, :]

    levels = []
    for m in (32, 16, 8, 4):
        n = C // (2 * m)
        b3d = b.reshape(n, 2 * m, HG_WIDTH)
        bp = jnp.broadcast_to(b3d[:, m - 1:m, :], (n, 2 * m, HG_WIDTH)).reshape(C, HG_WIDTH)
        levels.append((m, jnp.abs(b - bp)))
    lf_up = pltpu.roll(lf, C - 1, 0)
    lf_dn = pltpu.roll(lf, 1, 0)
    r4 = row & 3
    levels.append((2, jnp.where(r4 == 0, -lf_up, jnp.where(r4 == 1, 0.0, jnp.where(r4 == 2, -lf, -(lf + lf_dn))))))
    levels.append((1, jnp.where((row & 1) == 1, -lf, 0.0)))

    att = [jnp.where(tt == ss, jnp.sum(q[:, h * HG_DK:(h + 1) * HG_DK] * k[:, h * HG_DK:(h + 1) * HG_DK],
                                       axis=-1, keepdims=True), 0.0) for h in range(HG_HEADS)]
    for m, absd in levels:
        e = jnp.exp(-absd)
        q_side = (row & m) != 0
        qt = jnp.where(q_side, q * e, 0.0).astype(BF16)
        kt = jnp.where(q_side, 0.0, k * e).astype(BF16)
        same_group = (tt // (2 * m)) == (ss // (2 * m))
        for h in range(HG_HEADS):
            sl = slice(h * HG_DK, (h + 1) * HG_DK)
            att[h] = att[h] + jnp.where(same_group, _dot_nt(qt[:, sl], kt[:, sl]), 0.0)

    q_in = (q * jnp.exp(b)).astype(BF16)
    k_out = (k * jnp.exp(b_last - b)).astype(BF16)
    decay = jnp.exp(b_last)
    for h in range(HG_HEADS):
        sl = slice(h * HG_DK, (h + 1) * HG_DK)
        st = st_ref[h]
        o = _dot(att[h].astype(BF16), v[:, sl]) + _dot_nt(q_in[:, sl], st.astype(BF16))
        st_ref[h] = st * decay[:, sl] + _dot_tn(v[:, sl], k_out[:, sl])
        o = _rms_scale(o) * ng_ref[:, sl] * gs[:, sl]
        y_ref[rows, sl] = o.astype(BF16)


def _swa_block(j, first_block, band, kj, sink_ref, pb_ref, kv_ref, y_ref):
    W = WINDOW
    r0 = pl.multiple_of(j * W, W)
    valid = band & (kj >= jnp.where(first_block, W, 0))
    qb = pb_ref[pl.ds(r0, W), C_AQ:C_AQ + AT_WIDTH]
    kvb = kv_ref[pl.ds(r0, 2 * W), :]
    for hk in range(AT_KV_HEADS):
        kk = kvb[:, hk * AT_HEAD_DIM:(hk + 1) * AT_HEAD_DIM]
        vv = kvb[:, KV_WIDTH + hk * AT_HEAD_DIM:KV_WIDTH + (hk + 1) * AT_HEAD_DIM]
        for g in range(AT_GROUP):
            hq = hk * AT_GROUP + g
            sl = slice(hq * AT_HEAD_DIM, (hq + 1) * AT_HEAD_DIM)
            s = jnp.where(valid, _dot_nt(qb[:, sl], kk), MASK_VALUE)
            sink = sink_ref[hq]
            m = jnp.maximum(jnp.max(s, axis=-1, keepdims=True), sink)
            e = jnp.exp(s - m)
            den = jnp.sum(e, axis=-1, keepdims=True) + jnp.exp(sink - m)
            o = _dot(e.astype(BF16), vv) / den
            y_ref[pl.ds(r0, W), HG_WIDTH + hq * AT_HEAD_DIM:HG_WIDTH + (hq + 1) * AT_HEAD_DIM] = o.astype(BF16)


def _mixer_kernel(sink_ref, pb_ref, kvprev_ref, lf_ref, h_ref, wout_ref, ng_ref, o_ref,
                  st_ref, y_ref, kv_ref, *, ts):
    si = pl.program_id(1)

    @pl.when(si == 0)
    def _():
        st_ref[...] = jnp.zeros_like(st_ref)

    def chunk_body(c, carry):
        _hgrn_chunk(c, pb_ref, lf_ref, ng_ref, st_ref, y_ref)
        return carry

    lax.fori_loop(0, ts // HG_CHUNK, chunk_body, 0)

    W = WINDOW
    kv_ref[0:W, :] = kvprev_ref[...]
    kv_ref[W:W + ts, :] = pb_ref[:, C_AK:C_AK + 2 * KV_WIDTH]
    qi = lax.broadcasted_iota(jnp.int32, (W, 2 * W), 0)
    kj = lax.broadcasted_iota(jnp.int32, (W, 2 * W), 1)
    rel = kj - qi
    band = (rel >= 1) & (rel <= W)

    def blk_body(j, carry):
        first_block = (si * (ts // W) + j) == 0
        _swa_block(j, first_block, band, kj, sink_ref, pb_ref, kv_ref, y_ref)
        return carry

    lax.fori_loop(0, ts // W, blk_body, 0)

    o_ref[...] = h_ref[...] + _dot(y_ref[...], wout_ref[...])


def _mixer(pb, lf, h, w_out, ng, sinks, ts):
    B, S, _ = h.shape
    nblk = ts // WINDOW
    const = lambda b, s: (0, 0)
    return pl.pallas_call(
        functools.partial(_mixer_kernel, ts=ts),
        grid=(B, S // ts),
        in_specs=[
            pl.BlockSpec(memory_space=pltpu.SMEM),
            pl.BlockSpec((None, ts, IN_WIDTH), lambda b, s: (b, s, 0)),
            pl.BlockSpec((None, WINDOW, 2 * KV_WIDTH),
                         lambda b, s: (b, jnp.maximum(s * nblk - 1, 0), C_AK // (2 * KV_WIDTH))),
            pl.BlockSpec((None, ts, HG_WIDTH), lambda b, s: (b, s, 0)),
            pl.BlockSpec((None, ts, D_MODEL), lambda b, s: (b, s, 0)),
            pl.BlockSpec((D_MODEL, D_MODEL), const, pipeline_mode=pl.Buffered(1)),
            pl.BlockSpec((1, HG_WIDTH), const),
        ],
        out_specs=pl.BlockSpec((None, ts, D_MODEL), lambda b, s: (b, s, 0)),
        out_shape=jax.ShapeDtypeStruct((B, S, D_MODEL), F32),
        scratch_shapes=[
            pltpu.VMEM((HG_HEADS, HG_DK, HG_DK), F32),
            pltpu.VMEM((ts, D_MODEL), BF16),
            pltpu.VMEM((ts + WINDOW, 2 * KV_WIDTH), BF16),
        ],
        compiler_params=pltpu.CompilerParams(
            dimension_semantics=("arbitrary", "arbitrary"), vmem_limit_bytes=VMEM_LIMIT_BYTES),
        name="mixer",
    )(sinks, pb, pb, lf, h, w_out, ng)


def _ffn_kernel(h_ref, gf_ref, wup_ref, cw_ref, cb_ref, wdn_ref, gp_ref, wpg_ref, p_ref, wpu_ref,
                gfin_ref, o_ref, a_ref, gate_ref, *, ts, final_norm):
    si = pl.program_id(1)
    P = CONV_PAD

    @pl.when(si == 0)
    def _():
        a_ref[:, 0:P, :] = jnp.zeros((N_FF_CHUNKS, P, FF_CHUNK), F32)

    h = h_ref[...]
    u = (_rms_scale(h) * gf_ref[...]).astype(BF16)
    for c in range(N_FF_CHUNKS):
        cs = slice(c * FF_CHUNK, (c + 1) * FF_CHUNK)
        a = _dot(u, wup_ref[:, cs])
        b = _dot(u, wup_ref[:, D_FF + c * FF_CHUNK:D_FF + (c + 1) * FF_CHUNK])
        a_ref[c, P:P + ts, :] = a
        conv = (a * cw_ref[2:3, cs] + a_ref[c, P - 1:P - 1 + ts, :] * cw_ref[1:2, cs]
                + a_ref[c, P - 2:P - 2 + ts, :] * cw_ref[0:1, cs] + cb_ref[:, cs])
        gate_ref[:, cs] = (conv * _sigmoid(conv) * b).astype(BF16)
        a_ref[c, 0:P, :] = a_ref[c, ts:ts + P, :]
    h1 = h + _dot(gate_ref[...], wdn_ref[...])

    u2 = (_rms_scale(h1) * gp_ref[...]).astype(BF16)
    ple_gate = _sigmoid(_dot(u2, wpg_ref[...]))
    h2 = h1 + ple_gate * _dot(p_ref[...].astype(BF16), wpu_ref[...])
    if final_norm:
        h2 = _rms_scale(h2) * gfin_ref[...]
    o_ref[...] = h2


def _ffn(h, gf, w_up, cw, cb, w_dn, gp, wpg, p, wpu, gfin, ts, final_norm):
    B, S, _ = h.shape
    const = lambda b, s: (0, 0)
    resident = functools.partial(pl.BlockSpec, index_map=const, pipeline_mode=pl.Buffered(1))
    return pl.pallas_call(
        functools.partial(_ffn_kernel, ts=ts, final_norm=final_norm),
        grid=(B, S // ts),
        in_specs=[
            pl.BlockSpec((None, ts, D_MODEL), lambda b, s: (b, s, 0)),
            pl.BlockSpec((1, D_MODEL), const),
            resident((D_MODEL, 2 * D_FF)),
            pl.BlockSpec((3, D_FF), const),
            pl.BlockSpec((1, D_FF), const),
            resident((D_FF, D_MODEL)),
            pl.BlockSpec((1, D_MODEL), const),
            resident((D_MODEL, D_MODEL)),
            pl.BlockSpec((None, ts, PLE_DIM), lambda b, s: (b, s, 0)),
            resident((PLE_DIM, D_MODEL)),
            pl.BlockSpec((1, D_MODEL), const),
        ],
        out_specs=pl.BlockSpec((None, ts, D_MODEL), lambda b, s: (b, s, 0)),
        out_shape=jax.ShapeDtypeStruct((B, S, D_MODEL), F32),
        scratch_shapes=[
            pltpu.VMEM((N_FF_CHUNKS, CONV_PAD + ts, FF_CHUNK), F32),
            pltpu.VMEM((ts, D_FF), BF16),
        ],
        compiler_params=pltpu.CompilerParams(
            dimension_semantics=("arbitrary", "arbitrary"), vmem_limit_bytes=VMEM_LIMIT_BYTES),
        name="ffn_ple",
    )(h, gf, w_up, cw, cb, w_dn, gp, wpg, p, wpu, gfin)


def kernel(x, p, g_mix, w_in, lb_logits, hg_norm_g, attn_sinks, w_out, g_ffn, w_up, conv_w, conv_b,
           w_down, g_ple, w_ple_gate, w_ple_up, g_final):
    depth = w_in.shape[0]
    B, S, _ = x.shape
    ts = min(TOKEN_TILE, S)
    assert S % ts == 0 and ts % WINDOW == 0

    lb_p = jax.nn.softmax(lb_logits.astype(F32), axis=0)
    lb_all = jnp.cumsum(lb_p, axis=0) - lb_p[0]
    lbc = jnp.stack([jnp.log(jnp.maximum(lb_all, LB_FLOOR)), jnp.log1p(-lb_all), 1.0 - lb_all], axis=1)

    row = lambda a: a.reshape(1, -1).astype(F32)
    h = x
    for i in range(depth):
        pb, lf = _inproj(h, row(g_mix[i]), w_in[i].astype(BF16), lbc[i], ts)
        h = _mixer(pb, lf, h, w_out[i].astype(BF16), row(hg_norm_g[i]), attn_sinks[i].astype(F32), ts)
        h = _ffn(h, row(g_ffn[i]), w_up[i].astype(BF16), conv_w[i].astype(F32), row(conv_b[i]),
                 w_down[i].astype(BF16), row(g_ple[i]), w_ple_gate[i].astype(BF16), p[i],
                 w_ple_up[i].astype(BF16), row(g_final), ts, final_norm=(i == depth - 1))
    return h
```

```python
import functools

import jax
import jax.numpy as jnp
from jax import lax
from jax.experimental import pallas as pl
from jax.experimental.pallas import tpu as pltpu

F32 = jnp.float32
BF16 = jnp.bfloat16

D_MODEL = 1024
PLE_DIM = 256
HG_WIDTH = 512
HG_HEADS = 4
HG_DK = 128
HG_CHUNK = 64
AT_WIDTH = 512
AT_HEAD_DIM = 64
AT_Q_HEADS = 8
AT_KV_HEADS = 2
AT_GROUP = 4
WINDOW = 128
KV_WIDTH = AT_KV_HEADS * AT_HEAD_DIM
D_FF = 2816
EPS = 1e-6
MASK_VALUE = -1e30
LB_FLOOR = 1e-30
IN_WIDTH = 4 * HG_WIDTH + AT_WIDTH + 2 * KV_WIDTH

C_Q, C_K, C_V, C_G, C_AQ, C_AK, C_LH, C_LL, C_E1, C_E2 = 0, 512, 1024, 1536, 2048, 2560, 2816, 3328, 3840, 4352
PB_WIDTH = C_E2 + HG_WIDTH
LOG2E = 1.4426950408889634

FF_CHUNK = 256
N_FF_CHUNKS = D_FF // FF_CHUNK
CONV_PAD = 8

VMEM_LIMIT_BYTES = 56 * 1024 * 1024

TOKEN_TILE = 512


def _dot(a, b):
    return jnp.dot(a, b, preferred_element_type=F32)


def _dot_nt(a, b):
    return lax.dot_general(a, b, (((1,), (1,)), ((), ())), preferred_element_type=F32)


def _dot_tn(a, b):
    return lax.dot_general(a, b, (((0,), (0,)), ((), ())), preferred_element_type=F32)


def _sigmoid(x):
    return 1.0 / (1.0 + jnp.exp(-x))


def _rms_scale(x):
    return x * lax.rsqrt(jnp.mean(x * x, axis=-1, keepdims=True) + EPS)


def _inproj_kernel(h_ref, g_ref, w_ref, lbc_ref, pb_ref):
    u = (_rms_scale(h_ref[...]) * g_ref[...]).astype(BF16)

    def proj(lo, hi):
        return _dot(u, w_ref[:, lo:hi])

    hq = proj(0, 512)
    pb_ref[:, C_Q:C_Q + 512] = (hq * _sigmoid(hq)).astype(BF16)

    z = proj(512, 1024)
    log_lb = lbc_ref[0:1, :]
    log_1mlb = lbc_ref[1:2, :]
    one_mlb = lbc_ref[2:3, :]
    e = jnp.exp(-jnp.abs(z))
    log_sig = jnp.minimum(z, 0.0) - jnp.log1p(e)
    c = log_1mlb + log_sig
    log_f = jnp.maximum(log_lb, c) + jnp.log1p(jnp.exp(-jnp.abs(log_lb - c)))
    lf2 = log_f * LOG2E
    lf2_hi = lf2.astype(BF16)
    pb_ref[:, C_LH:C_LH + 512] = lf2_hi
    pb_ref[:, C_LL:C_LL + 512] = (lf2 - lf2_hi.astype(F32)).astype(BF16)
    sig_neg = jnp.where(z >= 0.0, e, 1.0) / (1.0 + e)
    pb_ref[:, C_K:C_K + 512] = (one_mlb * sig_neg).astype(BF16)
    f = lbc_ref[3:4, :] + one_mlb * (jnp.where(z >= 0.0, 1.0, e) / (1.0 + e))
    n = f.shape[0]
    r4 = lax.broadcasted_iota(jnp.int32, f.shape, 0) & 3
    f_next = pltpu.roll(f, n - 1, 0)
    f_prev = pltpu.roll(f, 1, 0)
    pb_ref[:, C_E1:C_E1 + 512] = jnp.where((r4 & 1) == 1, f, 1.0).astype(BF16)
    pb_ref[:, C_E2:C_E2 + 512] = jnp.where(
        r4 == 0, f_next, jnp.where(r4 == 1, 1.0, jnp.where(r4 == 2, f, f * f_prev))).astype(BF16)

    pb_ref[:, C_V:C_V + 512] = proj(1024, 1536).astype(BF16)
    pb_ref[:, C_G:C_G + 512] = _sigmoid(proj(1536, 2048)).astype(BF16)
    pb_ref[:, C_AQ:C_AQ + 512] = (proj(2048, 2560) * (AT_HEAD_DIM ** -0.5 * LOG2E)).astype(BF16)
    pb_ref[:, C_AK:C_AK + 256] = proj(2560, 2816).astype(BF16)


def _inproj(h, g, w, lbc, ts):
    B, S, _ = h.shape
    const = lambda b, s: (0, 0)
    return pl.pallas_call(
        _inproj_kernel,
        grid=(B, S // ts),
        in_specs=[
            pl.BlockSpec((None, ts, D_MODEL), lambda b, s: (b, s, 0)),
            pl.BlockSpec((1, D_MODEL), const),
            pl.BlockSpec((D_MODEL, IN_WIDTH), const, pipeline_mode=pl.Buffered(1)),
            pl.BlockSpec((4, HG_WIDTH), const),
        ],
        out_specs=pl.BlockSpec((None, ts, PB_WIDTH), lambda b, s: (b, s, 0)),
        out_shape=jax.ShapeDtypeStruct((B, S, PB_WIDTH), BF16),
        compiler_params=pltpu.CompilerParams(
            dimension_semantics=("arbitrary", "arbitrary"), vmem_limit_bytes=VMEM_LIMIT_BYTES),
        name="inproj",
    )(h, g, w, lbc)


HG_LEVELS = (32, 16, 8, 4, 2, 1)


def _row_blocks(x, split):
    lane = lax.broadcasted_iota(jnp.int32, x.shape, 1)
    zero = jnp.zeros_like(x)
    return jnp.concatenate([jnp.where(lane < split, x, zero), jnp.where(lane >= split, x, zero)], axis=0)


def _hgrn_consts():
    C = HG_CHUNK
    tt = lax.broadcasted_iota(jnp.int32, (C, C), 0)
    ss = lax.broadcasted_iota(jnp.int32, (C, C), 1)
    tril = jnp.where(tt >= ss, 1.0, 0.0).astype(BF16)
    level_masks = [((tt // (2 * m)) == (ss // (2 * m))) & ((tt & m) != 0) & ((ss & m) == 0) for m in HG_LEVELS]
    sub = lax.broadcasted_iota(jnp.int32, (C, HG_WIDTH), 0) & 7
    sign4 = jnp.where(sub >= 4, 1.0, -1.0)
    return tril, level_masks, tt == ss, sign4


def _hgrn_chunk(c, consts, pb_ref, b_ref, ng_ref, st_ref, y_ref):
    C = HG_CHUNK
    tril, level_masks, diag_mask, sign4 = consts
    rows = pl.ds(pl.multiple_of(c * C, C), C)
    q = pb_ref[rows, C_Q:C_Q + 512]
    k = pb_ref[rows, C_K:C_K + 512]
    v = pb_ref[rows, C_V:C_V + 512]
    gs = pb_ref[rows, C_G:C_G + 512]

    b = b_ref[rows, :]
    b_last = b[C - 1:C, :]

    factors = []
    for m in (32, 16, 8):
        parts = []
        for g in range(C // (2 * m)):
            lo = 2 * m * g
            bp = jnp.broadcast_to(b[lo + m - 1:lo + m, :], (m, HG_WIDTH))
            parts += [bp - b[lo:lo + m, :], b[lo + m:lo + 2 * m, :] - bp]
        factors.append(jnp.exp2(jnp.concatenate(parts, axis=0)).astype(BF16))
    bp4 = jnp.concatenate([jnp.broadcast_to(b[8 * g + 3:8 * g + 4, :], (8, HG_WIDTH)) for g in range(C // 8)], axis=0)
    factors.append(jnp.exp2((b - bp4) * sign4).astype(BF16))
    factors.append(pb_ref[rows, C_E2:C_E2 + 512])
    factors.append(pb_ref[rows, C_E1:C_E1 + 512])

    qk = q * k
    att = [jnp.where(diag_mask, jnp.sum(qk[:, h * HG_DK:(h + 1) * HG_DK].astype(F32), axis=-1, keepdims=True), 0.0)
           for h in range(HG_HEADS)]
    for e, mask in zip(factors, level_masks):
        qt = q * e
        kt = k * e
        for h in range(HG_HEADS):
            sl = slice(h * HG_DK, (h + 1) * HG_DK)
            att[h] = jnp.where(mask, _dot_nt(qt[:, sl], kt[:, sl]), att[h])

    q_in = q * jnp.exp2(b).astype(BF16)
    k_out = k * jnp.exp2(b_last - b).astype(BF16)
    decay = jnp.exp2(b_last)
    for h in range(HG_HEADS):
        sl = slice(h * HG_DK, (h + 1) * HG_DK)
        st = st_ref[h]
        o = _dot(att[h].astype(BF16), v[:, sl]) + _dot_nt(q_in[:, sl], st.astype(BF16))
        st_ref[h] = st * decay[:, sl] + _dot_tn(v[:, sl], k_out[:, sl])
        y_ref[rows, sl] = (_rms_scale(o) * ng_ref[:, sl]).astype(BF16) * gs[:, sl]


def _swa_consts():
    W = WINDOW
    qi = lax.broadcasted_iota(jnp.int32, (W, 4 * W), 0)
    kj = lax.broadcasted_iota(jnp.int32, (W, 4 * W), 1) & (2 * W - 1)
    rel = kj - qi
    band = (rel >= 1) & (rel <= W)
    lane = lax.broadcasted_iota(jnp.int32, (2 * W, W), 1)
    ones_rows = jnp.concatenate([jnp.where(lane < AT_HEAD_DIM, 1.0, 0.0),
                                 jnp.where(lane >= AT_HEAD_DIM, 1.0, 0.0)], axis=0).astype(BF16)
    return band, kj, ones_rows


def _swa_block(j, first_block, consts, sink_ref, pb_ref, kd_ref, vd_ref, y_ref):
    W = WINDOW
    D = AT_HEAD_DIM
    band, kj, ones_rows = consts
    r0 = pl.multiple_of(j * W, W)
    valid = band & (kj >= jnp.where(first_block, W, 0))
    lane = lax.broadcasted_iota(jnp.int32, (W, 2 * D), 1)
    for hk in range(AT_KV_HEADS):
        kd = kd_ref[pl.ds(r0, 2 * W), hk * 2 * D:(hk + 1) * 2 * D]
        vd = vd_ref[pl.ds(r0, 2 * W), hk * 2 * D:(hk + 1) * 2 * D]
        k_blocks = _row_blocks(kd, D)
        v_blocks = jnp.concatenate([_row_blocks(vd, D), ones_rows], axis=1)
        for pr in range(AT_GROUP // 2):
            c0 = (hk * AT_GROUP + 2 * pr) * D
            qp = pb_ref[pl.ds(r0, W), C_AQ + c0:C_AQ + c0 + 2 * D]
            s = jnp.where(valid, _dot_nt(qp, k_blocks), MASK_VALUE)
            sk = [sink_ref[hk * AT_GROUP + 2 * pr + i] * LOG2E for i in range(2)]
            m = [jnp.maximum(jnp.max(s[:, i * 2 * W:(i + 1) * 2 * W], axis=-1, keepdims=True), sk[i])
                 for i in range(2)]
            e = jnp.concatenate([jnp.exp2(s[:, i * 2 * W:(i + 1) * 2 * W] - m[i]) for i in range(2)], axis=1)
            o = _dot(e.astype(BF16), v_blocks)
            sink_term = jnp.where(lane < D, jnp.exp2(sk[0] - m[0]), jnp.exp2(sk[1] - m[1]))
            y_ref[pl.ds(r0, W), HG_WIDTH + c0:HG_WIDTH + c0 + 2 * D] = (
                o[:, 0:2 * D] / (o[:, 2 * D:4 * D] + sink_term)).astype(BF16)


def _dup_heads(x):
    lane = lax.broadcasted_iota(jnp.int32, x.shape, 1)
    swapped = pltpu.roll(x, AT_HEAD_DIM, 1)
    low = lane < AT_HEAD_DIM
    return jnp.concatenate([jnp.where(low, x, swapped), jnp.where(low, swapped, x)], axis=1)


def _mixer_kernel(sink_ref, pb_ref, kvprev_ref, h_ref, wout_ref, ng_ref, o_ref,
                  st_ref, y_ref, kd_ref, vd_ref, b_ref, *, ts):
    si = pl.program_id(1)
    W = WINDOW

    @pl.when(si == 0)
    def _():
        st_ref[...] = jnp.zeros_like(st_ref)

    hgrn_consts = _hgrn_consts()
    tril = hgrn_consts[0]
    for c in range(ts // HG_CHUNK):
        rows = slice(c * HG_CHUNK, (c + 1) * HG_CHUNK)
        bw = _dot(tril, pb_ref[rows, C_LH:C_LH + 1024])
        b_ref[rows, :] = bw[:, 0:512] + bw[:, 512:1024]

    def chunk_body(c, carry):
        _hgrn_chunk(c, hgrn_consts, pb_ref, b_ref, ng_ref, st_ref, y_ref)
        return carry

    lax.fori_loop(0, ts // HG_CHUNK, chunk_body, 0, unroll=True)

    kd_ref[0:W, :] = _dup_heads(kvprev_ref[:, 0:KV_WIDTH].astype(F32)).astype(BF16)
    vd_ref[0:W, :] = _dup_heads(kvprev_ref[:, KV_WIDTH:2 * KV_WIDTH].astype(F32)).astype(BF16)
    kd_ref[W:W + ts, :] = _dup_heads(pb_ref[:, C_AK:C_AK + KV_WIDTH].astype(F32)).astype(BF16)
    vd_ref[W:W + ts, :] = _dup_heads(pb_ref[:, C_AK + KV_WIDTH:C_AK + 2 * KV_WIDTH].astype(F32)).astype(BF16)
    swa_consts = _swa_consts()

    def blk_body(j, carry):
        first_block = (si * (ts // W) + j) == 0
        _swa_block(j, first_block, swa_consts, sink_ref, pb_ref, kd_ref, vd_ref, y_ref)
        return carry

    lax.fori_loop(0, ts // W, blk_body, 0, unroll=True)

    o_ref[...] = h_ref[...] + _dot(y_ref[...], wout_ref[...])


def _mixer(pb, h, w_out, ng, sinks, ts):
    B, S, _ = h.shape
    nblk = ts // WINDOW
    const = lambda b, s: (0, 0)
    return pl.pallas_call(
        functools.partial(_mixer_kernel, ts=ts),
        grid=(B, S // ts),
        in_specs=[
            pl.BlockSpec(memory_space=pltpu.SMEM),
            pl.BlockSpec((None, ts, PB_WIDTH), lambda b, s: (b, s, 0)),
            pl.BlockSpec((None, WINDOW, 2 * KV_WIDTH),
                         lambda b, s: (b, jnp.maximum(s * nblk - 1, 0), C_AK // (2 * KV_WIDTH))),
            pl.BlockSpec((None, ts, D_MODEL), lambda b, s: (b, s, 0)),
            pl.BlockSpec((D_MODEL, D_MODEL), const, pipeline_mode=pl.Buffered(1)),
            pl.BlockSpec((1, HG_WIDTH), const),
        ],
        out_specs=pl.BlockSpec((None, ts, D_MODEL), lambda b, s: (b, s, 0)),
        out_shape=jax.ShapeDtypeStruct((B, S, D_MODEL), F32),
        scratch_shapes=[
            pltpu.VMEM((HG_HEADS, HG_DK, HG_DK), F32),
            pltpu.VMEM((ts, D_MODEL), BF16),
            pltpu.VMEM((ts + WINDOW, 2 * KV_WIDTH), BF16),
            pltpu.VMEM((ts + WINDOW, 2 * KV_WIDTH), BF16),
            pltpu.VMEM((ts, HG_WIDTH), F32),
        ],
        compiler_params=pltpu.CompilerParams(
            dimension_semantics=("arbitrary", "arbitrary"), vmem_limit_bytes=VMEM_LIMIT_BYTES),
        name="mixer",
    )(sinks, pb, pb, h, w_out, ng)


def _ffn_kernel(h_ref, gf_ref, wup_ref, cw_ref, cb_ref, wdn_ref, gp_ref, wpg_ref, p_ref, wpu_ref,
                gfin_ref, o_ref, a_ref, gate_ref, *, ts, final_norm):
    si = pl.program_id(1)
    P = CONV_PAD

    @pl.when(si == 0)
    def _():
        a_ref[:, 0:P, :] = jnp.zeros((N_FF_CHUNKS, P, FF_CHUNK), F32)

    h = h_ref[...]
    u = (_rms_scale(h) * gf_ref[...]).astype(BF16)
    for c in range(N_FF_CHUNKS):
        cs = slice(c * FF_CHUNK, (c + 1) * FF_CHUNK)
        a = _dot(u, wup_ref[:, cs])
        b = _dot(u, wup_ref[:, D_FF + c * FF_CHUNK:D_FF + (c + 1) * FF_CHUNK])
        a_ref[c, P:P + ts, :] = a
        conv = (a * cw_ref[2:3, cs] + a_ref[c, P - 1:P - 1 + ts, :] * cw_ref[1:2, cs]
                + a_ref[c, P - 2:P - 2 + ts, :] * cw_ref[0:1, cs] + cb_ref[:, cs])
        gate_ref[:, cs] = (conv * _sigmoid(conv) * b).astype(BF16)
        a_ref[c, 0:P, :] = a_ref[c, ts:ts + P, :]
    h1 = h + _dot(gate_ref[...], wdn_ref[...])

    u2 = (_rms_scale(h1) * gp_ref[...]).astype(BF16)
    ple_gate = _sigmoid(_dot(u2, wpg_ref[...]))
    h2 = h1 + ple_gate * _dot(p_ref[...].astype(BF16), wpu_ref[...])
    if final_norm:
        h2 = _rms_scale(h2) * gfin_ref[...]
    o_ref[...] = h2


def _ffn(h, gf, w_up, cw, cb, w_dn, gp, wpg, p, wpu, gfin, ts, final_norm):
    B, S, _ = h.shape
    const = lambda b, s: (0, 0)
    resident = functools.partial(pl.BlockSpec, index_map=const, pipeline_mode=pl.Buffered(1))
    return pl.pallas_call(
        functools.partial(_ffn_kernel, ts=ts, final_norm=final_norm),
        grid=(B, S // ts),
        in_specs=[
            pl.BlockSpec((None, ts, D_MODEL), lambda b, s: (b, s, 0)),
            pl.BlockSpec((1, D_MODEL), const),
            resident((D_MODEL, 2 * D_FF)),
            pl.BlockSpec((3, D_FF), const),
            pl.BlockSpec((1, D_FF), const),
            resident((D_FF, D_MODEL)),
            pl.BlockSpec((1, D_MODEL), const),
            resident((D_MODEL, D_MODEL)),
            pl.BlockSpec((None, ts, PLE_DIM), lambda b, s: (b, s, 0)),
            resident((PLE_DIM, D_MODEL)),
            pl.BlockSpec((1, D_MODEL), const),
        ],
        out_specs=pl.BlockSpec((None, ts, D_MODEL), lambda b, s: (b, s, 0)),
        out_shape=jax.ShapeDtypeStruct((B, S, D_MODEL), F32),
        scratch_shapes=[
            pltpu.VMEM((N_FF_CHUNKS, CONV_PAD + ts, FF_CHUNK), F32),
            pltpu.VMEM((ts, D_FF), BF16),
        ],
        compiler_params=pltpu.CompilerParams(
            dimension_semantics=("arbitrary", "arbitrary"), vmem_limit_bytes=VMEM_LIMIT_BYTES),
        name="ffn_ple",
    )(h, gf, w_up, cw, cb, w_dn, gp, wpg, p, wpu, gfin)


def kernel(x, p, g_mix, w_in, lb_logits, hg_norm_g, attn_sinks, w_out, g_ffn, w_up, conv_w, conv_b,
           w_down, g_ple, w_ple_gate, w_ple_up, g_final):
    depth = w_in.shape[0]
    B, S, _ = x.shape
    ts = min(TOKEN_TILE, S)
    assert S % ts == 0 and ts % WINDOW == 0

    lb_p = jax.nn.softmax(lb_logits.astype(F32), axis=0)
    lb_all = jnp.cumsum(lb_p, axis=0) - lb_p[0]
    lb_floor = jnp.maximum(lb_all, LB_FLOOR)
    lbc = jnp.stack([jnp.log(lb_floor), jnp.log1p(-lb_all), 1.0 - lb_all, lb_floor], axis=1)

    row = lambda a: a.reshape(1, -1).astype(F32)
    h = x
    for i in range(depth):
        pb = _inproj(h, row(g_mix[i]), w_in[i].astype(BF16), lbc[i], ts)
        h = _mixer(pb, h, w_out[i].astype(BF16), row(hg_norm_g[i]), attn_sinks[i].astype(F32), ts)
        h = _ffn(h, row(g_ffn[i]), w_up[i].astype(BF16), conv_w[i].astype(F32), row(conv_b[i]),
                 w_down[i].astype(BF16), row(g_ple[i]), w_ple_gate[i].astype(BF16), p[i],
                 w_ple_up[i].astype(BF16), row(g_final), ts, final_norm=(i == depth - 1))
    return h
```

```python
import functools

import jax
import jax.numpy as jnp
from jax import lax
from jax.experimental import pallas as pl
from jax.experimental.pallas import tpu as pltpu

F32 = jnp.float32
BF16 = jnp.bfloat16

D_MODEL = 1024
PLE_DIM = 256
HG_WIDTH = 512
HG_HEADS = 4
HG_DK = 128
HG_CHUNK = 64
AT_WIDTH = 512
AT_HEAD_DIM = 64
AT_Q_HEADS = 8
AT_KV_HEADS = 2
AT_GROUP = 4
WINDOW = 128
KV_WIDTH = AT_KV_HEADS * AT_HEAD_DIM
D_FF = 2816
EPS = 1e-6
MASK_VALUE = -1e30
LB_FLOOR = 1e-30
IN_WIDTH = 4 * HG_WIDTH + AT_WIDTH + 2 * KV_WIDTH

C_Q, C_K, C_V, C_G, C_AQ, C_AK, C_LH, C_LL, C_E1, C_E2 = 0, 512, 1024, 1536, 2048, 2560, 2816, 3328, 3840, 4352
PB_WIDTH = C_E2 + HG_WIDTH
LOG2E = 1.4426950408889634

FF_CHUNK = 256
N_FF_CHUNKS = D_FF // FF_CHUNK
CONV_PAD = 8

VMEM_LIMIT_BYTES = 56 * 1024 * 1024

TOKEN_TILE = 512
EPILOGUE_ROWS = 16
PROJ_PIECE = 256


def _dot(a, b):
    return jnp.dot(a, b, preferred_element_type=F32)


def _dot_nt(a, b):
    return lax.dot_general(a, b, (((1,), (1,)), ((), ())), preferred_element_type=F32)


def _dot_tn(a, b):
    return lax.dot_general(a, b, (((0,), (0,)), ((), ())), preferred_element_type=F32)


def _sigmoid(x):
    return 1.0 / (1.0 + jnp.exp(-x))


def _rms_scale(x):
    return x * lax.rsqrt(jnp.mean(x * x, axis=-1, keepdims=True) + EPS)


def _inproj_kernel(h_ref, g_ref, w_ref, lbc_ref, pb_ref, *x_refs):
    u = (_rms_scale(h_ref[...]) * g_ref[...]).astype(BF16)

    def proj(lo, hi):
        return _dot(u, w_ref[:, lo:hi])

    n = u.shape[0]
    R = EPILOGUE_ROWS
    PW = PROJ_PIECE
    r4 = lax.broadcasted_iota(jnp.int32, (R, PW), 0) & 3

    def gate_epilogue(r, z, j):
        rows = slice(r, r + R)
        cs = slice(j, j + PW)
        log2_lb, log2_1mlb, one_mlb, lb_floor = (lbc_ref[i:i + 1, cs] for i in range(4))
        z2 = z * LOG2E
        e = jnp.exp2(-jnp.abs(z2))
        inv = 1.0 / (1.0 + e)
        c = log2_1mlb + (jnp.minimum(z2, 0.0) - jnp.log2(1.0 + e))
        lf2 = jnp.maximum(log2_lb, c) + jnp.log2(1.0 + jnp.exp2(-jnp.abs(log2_lb - c)))
        lf2_hi = lf2.astype(BF16)
        pb_ref[rows, C_LH + j:C_LH + j + PW] = lf2_hi
        pb_ref[rows, C_LL + j:C_LL + j + PW] = (lf2 - lf2_hi.astype(F32)).astype(BF16)
        pos = z >= 0.0
        pb_ref[rows, C_K + j:C_K + j + PW] = (one_mlb * (jnp.where(pos, e, 1.0) * inv)).astype(BF16)
        f = lb_floor + one_mlb * (jnp.where(pos, 1.0, e) * inv)
        f_next = pltpu.roll(f, R - 1, 0)
        f_prev = pltpu.roll(f, 1, 0)
        pb_ref[rows, C_E1 + j:C_E1 + j + PW] = jnp.where((r4 & 1) == 1, f, 1.0).astype(BF16)
        e2 = jnp.where(r4 == 0, f_next, jnp.where(r4 == 1, 1.0, jnp.where(r4 == 2, f, f * f_prev)))
        pb_ref[rows, C_E2 + j:C_E2 + j + PW] = e2.astype(BF16)
        return e2

    def store(col, fn):
        def epilogue(r, x, j):
            y = fn(x)
            pb_ref[r:r + R, col + j:col + j + PW] = y.astype(BF16)
            return y
        return epilogue

    slabs = [
        (512, 512, gate_epilogue),
        (0, 512, store(C_Q, lambda x: x * _sigmoid(x))),
        (1536, 512, store(C_G, _sigmoid)),
        (1024, 512, store(C_V, lambda x: x)),
        (2048, 512, store(C_AQ, lambda x: x * (AT_HEAD_DIM ** -0.5 * LOG2E))),
        (2560, 256, store(C_AK, lambda x: x)),
    ]
    cols = [(base + j, j, epi) for base, width, epi in slabs for j in range(0, width, PW)]
    pieces = [(lo, j, epi, x_ref) for (lo, j, epi), x_ref in zip(cols, x_refs, strict=True)]

    zero = jnp.minimum(pl.program_id(0), 0)

    def tasks(piece):
        _, j, epi, x_ref = piece
        return [functools.partial(
            lambda r, j, epi, x_ref: epi(r, x_ref[pl.ds(pl.multiple_of(zero + r, R), R), :], j), r, j, epi, x_ref)
            for r in range(0, n, R)]

    n_gate = slabs[0][1] // PW
    heavy = [t for p in pieces[:n_gate] for t in tasks(p)]
    n_later = len(pieces) - n_gate
    share = -(-len(heavy) // n_later)
    for i, (lo, j, epi, x_ref) in enumerate(pieces):
        x_ref[...] = proj(lo, lo + PW)
        if i >= n_gate:
            for t in heavy[(i - n_gate) * share:(i - n_gate + 1) * share]:
                t()
        if i > n_gate:
            for t in tasks(pieces[i - 1]):
                t()
    for t in tasks(pieces[-1]):
        t()


def _layer_weight(layer, rows, cols):
    return pl.BlockSpec((None, rows, cols), lambda b, s: (layer, 0, 0), pipeline_mode=pl.Buffered(1))


def _inproj(h, g, w, lbc, ts, layer):
    B, S, _ = h.shape
    const = lambda b, s: (0, 0)
    return pl.pallas_call(
        _inproj_kernel,
        grid=(B, S // ts),
        in_specs=[
            pl.BlockSpec((None, ts, D_MODEL), lambda b, s: (b, s, 0)),
            pl.BlockSpec((1, D_MODEL), const),
            _layer_weight(layer, D_MODEL, IN_WIDTH),
            pl.BlockSpec((4, HG_WIDTH), const),
        ],
        out_specs=pl.BlockSpec((None, ts, PB_WIDTH), lambda b, s: (b, s, 0)),
        out_shape=jax.ShapeDtypeStruct((B, S, PB_WIDTH), BF16),
        scratch_shapes=[pltpu.VMEM((ts, PROJ_PIECE), F32)] * (IN_WIDTH // PROJ_PIECE),
        compiler_params=pltpu.CompilerParams(
            dimension_semantics=("arbitrary", "arbitrary"), vmem_limit_bytes=VMEM_LIMIT_BYTES),
        name="inproj",
    )(h, g, w, lbc)


HG_LEVELS = (32, 16, 8, 4, 2, 1)


def _row_blocks(x, split):
    lane = lax.broadcasted_iota(jnp.int32, x.shape, 1)
    zero = jnp.zeros_like(x)
    return jnp.concatenate([jnp.where(lane < split, x, zero), jnp.where(lane >= split, x, zero)], axis=0)


def _hgrn_consts():
    C = HG_CHUNK
    tt = lax.broadcasted_iota(jnp.int32, (C, C), 0)
    ss = lax.broadcasted_iota(jnp.int32, (C, C), 1)
    tril = jnp.where(tt >= ss, 1.0, 0.0).astype(BF16)
    level_masks = [((tt // (2 * m)) == (ss // (2 * m))) & ((tt & m) != 0) & ((ss & m) == 0) for m in HG_LEVELS]
    sub = lax.broadcasted_iota(jnp.int32, (C, HG_WIDTH), 0) & 7
    sign4 = jnp.where(sub >= 4, 1.0, -1.0)
    return tril, level_masks, tt == ss, sign4


def _hgrn_chunk(c, consts, pb_ref, b_ref, ng_ref, st_ref, y_ref):
    C = HG_CHUNK
    tril, level_masks, diag_mask, sign4 = consts
    rows = pl.ds(c * C, C)
    q = pb_ref[rows, C_Q:C_Q + 512]
    k = pb_ref[rows, C_K:C_K + 512]
    v = pb_ref[rows, C_V:C_V + 512]
    gs = pb_ref[rows, C_G:C_G + 512]

    b = b_ref[rows, :]
    b_last = b[C - 1:C, :]

    factors = []
    for m in (32, 16, 8):
        parts = []
        for g in range(C // (2 * m)):
            lo = 2 * m * g
            bp = jnp.broadcast_to(b[lo + m - 1:lo + m, :], (m, HG_WIDTH))
            parts += [bp - b[lo:lo + m, :], b[lo + m:lo + 2 * m, :] - bp]
        factors.append(jnp.exp2(jnp.concatenate(parts, axis=0)).astype(BF16))
    bp4 = jnp.concatenate([jnp.broadcast_to(b[8 * g + 3:8 * g + 4, :], (8, HG_WIDTH)) for g in range(C // 8)], axis=0)
    factors.append(jnp.exp2((b - bp4) * sign4).astype(BF16))
    factors.append(pb_ref[rows, C_E2:C_E2 + 512])
    factors.append(pb_ref[rows, C_E1:C_E1 + 512])

    qk = q * k
    att = [jnp.where(diag_mask, jnp.sum(qk[:, h * HG_DK:(h + 1) * HG_DK].astype(F32), axis=-1, keepdims=True), 0.0)
           for h in range(HG_HEADS)]
    for e, mask in zip(factors, level_masks):
        qt = q * e
        kt = k * e
        for h in range(HG_HEADS):
            sl = slice(h * HG_DK, (h + 1) * HG_DK)
            att[h] = jnp.where(mask, _dot_nt(qt[:, sl], kt[:, sl]), att[h])

    q_in = q * jnp.exp2(b).astype(BF16)
    k_out = k * jnp.exp2(b_last - b).astype(BF16)
    decay = jnp.exp2(b_last)
    for h in range(HG_HEADS):
        sl = slice(h * HG_DK, (h + 1) * HG_DK)
        st = st_ref[h]
        o = _dot(att[h].astype(BF16), v[:, sl]) + _dot_nt(q_in[:, sl], st.astype(BF16))
        st_ref[h] = st * decay[:, sl] + _dot_tn(v[:, sl], k_out[:, sl])
        y_ref[rows, sl] = (_rms_scale(o) * ng_ref[:, sl]).astype(BF16) * gs[:, sl]


def _swa_consts():
    W = WINDOW
    qi = lax.broadcasted_iota(jnp.int32, (W, 4 * W), 0)
    kj = lax.broadcasted_iota(jnp.int32, (W, 4 * W), 1) & (2 * W - 1)
    rel = kj - qi
    band = (rel >= 1) & (rel <= W)
    lane = lax.broadcasted_iota(jnp.int32, (2 * W, W), 1)
    ones_rows = jnp.concatenate([jnp.where(lane < AT_HEAD_DIM, 1.0, 0.0),
                                 jnp.where(lane >= AT_HEAD_DIM, 1.0, 0.0)], axis=0).astype(BF16)
    return band, kj, ones_rows


def _swa_block(j, first_block, consts, sink_ref, pb_ref, kd_ref, vd_ref, y_ref):
    W = WINDOW
    D = AT_HEAD_DIM
    band, kj, ones_rows = consts
    r0 = j * W
    valid = band & (kj >= jnp.where(first_block, W, 0))
    lane = lax.broadcasted_iota(jnp.int32, (W, 2 * D), 1)
    for hk in range(AT_KV_HEADS):
        kd = kd_ref[pl.ds(r0, 2 * W), hk * 2 * D:(hk + 1) * 2 * D]
        vd = vd_ref[pl.ds(r0, 2 * W), hk * 2 * D:(hk + 1) * 2 * D]
        k_blocks = _row_blocks(kd, D)
        v_blocks = jnp.concatenate([_row_blocks(vd, D), ones_rows], axis=1)
        for pr in range(AT_GROUP // 2):
            c0 = (hk * AT_GROUP + 2 * pr) * D
            qp = pb_ref[pl.ds(r0, W), C_AQ + c0:C_AQ + c0 + 2 * D]
            s = jnp.where(valid, _dot_nt(qp, k_blocks), MASK_VALUE)
            sk = [sink_ref[hk * AT_GROUP + 2 * pr + i] * LOG2E for i in range(2)]
            m = [jnp.maximum(jnp.max(s[:, i * 2 * W:(i + 1) * 2 * W], axis=-1, keepdims=True), sk[i])
                 for i in range(2)]
            e = jnp.concatenate([jnp.exp2(s[:, i * 2 * W:(i + 1) * 2 * W] - m[i]) for i in range(2)], axis=1)
            o = _dot(e.astype(BF16), v_blocks)
            sink_term = jnp.where(lane < D, jnp.exp2(sk[0] - m[0]), jnp.exp2(sk[1] - m[1]))
            y_ref[pl.ds(r0, W), HG_WIDTH + c0:HG_WIDTH + c0 + 2 * D] = (
                o[:, 0:2 * D] / (o[:, 2 * D:4 * D] + sink_term)).astype(BF16)


def _dup_heads(x):
    lane = lax.broadcasted_iota(jnp.int32, x.shape, 1)
    swapped = pltpu.roll(x, AT_HEAD_DIM, 1)
    low = lane < AT_HEAD_DIM
    return jnp.concatenate([jnp.where(low, x, swapped), jnp.where(low, swapped, x)], axis=1)


def _mixer_kernel(sink_ref, pb_ref, kvprev_ref, h_ref, wout_ref, ng_ref, o_ref,
                  st_ref, y_ref, kd_ref, vd_ref, b_ref, *, ts):
    si = pl.program_id(1)
    W = WINDOW

    @pl.when(si == 0)
    def _():
        st_ref[...] = jnp.zeros_like(st_ref)

    hgrn_consts = _hgrn_consts()
    tril = hgrn_consts[0]
    for c in range(ts // HG_CHUNK):
        rows = slice(c * HG_CHUNK, (c + 1) * HG_CHUNK)
        bw = _dot(tril, pb_ref[rows, C_LH:C_LH + 1024])
        b_ref[rows, :] = bw[:, 0:512] + bw[:, 512:1024]

    kd_ref[0:W, :] = _dup_heads(kvprev_ref[:, 0:KV_WIDTH].astype(F32)).astype(BF16)
    vd_ref[0:W, :] = _dup_heads(kvprev_ref[:, KV_WIDTH:2 * KV_WIDTH].astype(F32)).astype(BF16)
    kd_ref[W:W + ts, :] = _dup_heads(pb_ref[:, C_AK:C_AK + KV_WIDTH].astype(F32)).astype(BF16)
    vd_ref[W:W + ts, :] = _dup_heads(pb_ref[:, C_AK + KV_WIDTH:C_AK + 2 * KV_WIDTH].astype(F32)).astype(BF16)
    swa_consts = _swa_consts()

    for c in range(ts // HG_CHUNK):
        _hgrn_chunk(c, hgrn_consts, pb_ref, b_ref, ng_ref, st_ref, y_ref)
    for j in range(ts // W):
        first_block = (si * (ts // W) + j) == 0
        _swa_block(j, first_block, swa_consts, sink_ref, pb_ref, kd_ref, vd_ref, y_ref)

    o_ref[...] = h_ref[...] + _dot(y_ref[...], wout_ref[...])


def _mixer(pb, h, w_out, ng, sinks, ts, layer):
    B, S, _ = h.shape
    nblk = ts // WINDOW
    const = lambda b, s: (0, 0)
    return pl.pallas_call(
        functools.partial(_mixer_kernel, ts=ts),
        grid=(B, S // ts),
        in_specs=[
            pl.BlockSpec(memory_space=pltpu.SMEM),
            pl.BlockSpec((None, ts, PB_WIDTH), lambda b, s: (b, s, 0)),
            pl.BlockSpec((None, WINDOW, 2 * KV_WIDTH),
                         lambda b, s: (b, jnp.maximum(s * nblk - 1, 0), C_AK // (2 * KV_WIDTH))),
            pl.BlockSpec((None, ts, D_MODEL), lambda b, s: (b, s, 0)),
            _layer_weight(layer, D_MODEL, D_MODEL),
            pl.BlockSpec((1, HG_WIDTH), const),
        ],
        out_specs=pl.BlockSpec((None, ts, D_MODEL), lambda b, s: (b, s, 0)),
        out_shape=jax.ShapeDtypeStruct((B, S, D_MODEL), F32),
        scratch_shapes=[
            pltpu.VMEM((HG_HEADS, HG_DK, HG_DK), F32),
            pltpu.VMEM((ts, D_MODEL), BF16),
            pltpu.VMEM((ts + WINDOW, 2 * KV_WIDTH), BF16),
            pltpu.VMEM((ts + WINDOW, 2 * KV_WIDTH), BF16),
            pltpu.VMEM((ts, HG_WIDTH), F32),
        ],
        compiler_params=pltpu.CompilerParams(
            dimension_semantics=("arbitrary", "arbitrary"), vmem_limit_bytes=VMEM_LIMIT_BYTES),
        name="mixer",
    )(sinks, pb, pb, h, w_out, ng)


def _ffn_kernel(h_ref, gf_ref, wup_ref, cw_ref, cb_ref, wdn_ref, gp_ref, wpg_ref, p_ref, wpu_ref,
                gfin_ref, o_ref, a_ref, gate_ref, *, ts, final_norm):
    si = pl.program_id(1)
    P = CONV_PAD

    @pl.when(si == 0)
    def _():
        a_ref[:, 0:P, :] = jnp.zeros((N_FF_CHUNKS, P, FF_CHUNK), F32)

    h = h_ref[...]
    u = (_rms_scale(h) * gf_ref[...]).astype(BF16)
    for c in range(N_FF_CHUNKS):
        cs = slice(c * FF_CHUNK, (c + 1) * FF_CHUNK)
        a = _dot(u, wup_ref[:, cs])
        b = _dot(u, wup_ref[:, D_FF + c * FF_CHUNK:D_FF + (c + 1) * FF_CHUNK])
        a_ref[c, P:P + ts, :] = a
        conv = (a * cw_ref[2:3, cs] + a_ref[c, P - 1:P - 1 + ts, :] * cw_ref[1:2, cs]
                + a_ref[c, P - 2:P - 2 + ts, :] * cw_ref[0:1, cs] + cb_ref[:, cs])
        gate_ref[:, cs] = (conv * _sigmoid(conv) * b).astype(BF16)
        a_ref[c, 0:P, :] = a_ref[c, ts:ts + P, :]
    h1 = h + _dot(gate_ref[...], wdn_ref[...])

    u2 = (_rms_scale(h1) * gp_ref[...]).astype(BF16)
    ple_gate = _sigmoid(_dot(u2, wpg_ref[...]))
    h2 = h1 + ple_gate * _dot(p_ref[...].astype(BF16), wpu_ref[...])
    if final_norm:
        h2 = _rms_scale(h2) * gfin_ref[...]
    o_ref[...] = h2


def _ffn(h, gf, w_up, cw, cb, w_dn, gp, wpg, p, wpu, gfin, ts, layer, final_norm):
    B, S, _ = h.shape
    const = lambda b, s: (0, 0)
    resident = functools.partial(_layer_weight, layer)
    return pl.pallas_call(
        functools.partial(_ffn_kernel, ts=ts, final_norm=final_norm),
        grid=(B, S // ts),
        in_specs=[
            pl.BlockSpec((None, ts, D_MODEL), lambda b, s: (b, s, 0)),
            pl.BlockSpec((1, D_MODEL), const),
            resident(D_MODEL, 2 * D_FF),
            pl.BlockSpec((3, D_FF), const),
            pl.BlockSpec((1, D_FF), const),
            resident(D_FF, D_MODEL),
            pl.BlockSpec((1, D_MODEL), const),
            resident(D_MODEL, D_MODEL),
            pl.BlockSpec((None, None, ts, PLE_DIM), lambda b, s: (layer, b, s, 0)),
            resident(PLE_DIM, D_MODEL),
            pl.BlockSpec((1, D_MODEL), const),
        ],
        out_specs=pl.BlockSpec((None, ts, D_MODEL), lambda b, s: (b, s, 0)),
        out_shape=jax.ShapeDtypeStruct((B, S, D_MODEL), F32),
        scratch_shapes=[
            pltpu.VMEM((N_FF_CHUNKS, CONV_PAD + ts, FF_CHUNK), F32),
            pltpu.VMEM((ts, D_FF), BF16),
        ],
        compiler_params=pltpu.CompilerParams(
            dimension_semantics=("arbitrary", "arbitrary"), vmem_limit_bytes=VMEM_LIMIT_BYTES),
        name="ffn_ple",
    )(h, gf, w_up, cw, cb, w_dn, gp, wpg, p, wpu, gfin)


def kernel(x, p, g_mix, w_in, lb_logits, hg_norm_g, attn_sinks, w_out, g_ffn, w_up, conv_w, conv_b,
           w_down, g_ple, w_ple_gate, w_ple_up, g_final):
    depth = w_in.shape[0]
    B, S, _ = x.shape
    ts = min(TOKEN_TILE, S)
    assert S % ts == 0 and ts % WINDOW == 0

    lb_p = jax.nn.softmax(lb_logits.astype(F32), axis=0)
    lb_all = jnp.cumsum(lb_p, axis=0) - lb_p[0]
    lb_floor = jnp.maximum(lb_all, LB_FLOOR)
    lbc = jnp.stack([jnp.log(lb_floor) * LOG2E, jnp.log1p(-lb_all) * LOG2E, 1.0 - lb_all, lb_floor], axis=1)

    w_in, w_out, w_up, w_down, w_ple_gate, w_ple_up = (
        w.astype(BF16) for w in (w_in, w_out, w_up, w_down, w_ple_gate, w_ple_up))
    row = lambda a: a.reshape(1, -1).astype(F32)
    h = x
    for i in range(depth):
        pb = _inproj(h, row(g_mix[i]), w_in, lbc[i], ts, i)
        h = _mixer(pb, h, w_out, row(hg_norm_g[i]), attn_sinks[i].astype(F32), ts, i)
        h = _ffn(h, row(g_ffn[i]), w_up, conv_w[i].astype(F32), row(conv_b[i]), w_down, row(g_ple[i]),
                 w_ple_gate, p, w_ple_up, row(g_final), ts, i, final_norm=(i == depth - 1))
    return h
```

```python
import functools

import jax
import jax.numpy as jnp
from jax import lax
from jax.experimental import pallas as pl
from jax.experimental.pallas import tpu as pltpu

F32 = jnp.float32
BF16 = jnp.bfloat16

D_MODEL = 1024
PLE_DIM = 256
HG_WIDTH = 512
HG_HEADS = 4
HG_DK = 128
HG_CHUNK = 64
AT_WIDTH = 512
AT_HEAD_DIM = 64
AT_Q_HEADS = 8
AT_KV_HEADS = 2
AT_GROUP = 4
WINDOW = 128
KV_WIDTH = AT_KV_HEADS * AT_HEAD_DIM
D_FF = 2816
EPS = 1e-6
MASK_VALUE = -1e30
LB_FLOOR = 1e-30
IN_WIDTH = 4 * HG_WIDTH + AT_WIDTH + 2 * KV_WIDTH

C_Q, C_K, C_V, C_G, C_AQ, C_AK, C_LH, C_LL, C_E1, C_E2 = 0, 512, 1024, 1536, 2048, 2560, 2816, 3328, 3840, 4352
PB_WIDTH = C_E2 + HG_WIDTH
LOG2E = 1.4426950408889634

FF_CHUNK = 256
N_FF_CHUNKS = D_FF // FF_CHUNK
CONV_PAD = 8

VMEM_LIMIT_BYTES = 56 * 1024 * 1024

TOKEN_TILE = 512
EPILOGUE_ROWS = 16
PROJ_PIECE = 256


def _dot(a, b):
    return jnp.dot(a, b, preferred_element_type=F32)


def _dot_nt(a, b):
    return lax.dot_general(a, b, (((1,), (1,)), ((), ())), preferred_element_type=F32)


def _dot_tn(a, b):
    return lax.dot_general(a, b, (((0,), (0,)), ((), ())), preferred_element_type=F32)


def _sigmoid(x):
    return 1.0 / (1.0 + jnp.exp(-x))


def _rms_scale(x):
    return x * lax.rsqrt(jnp.mean(x * x, axis=-1, keepdims=True) + EPS)


def _inproj_kernel(h_ref, g_ref, w_ref, lbc_ref, pb_ref, *x_refs):
    u = (_rms_scale(h_ref[...]) * g_ref[...]).astype(BF16)

    def proj(lo, hi):
        return _dot(u, w_ref[:, lo:hi])

    n = u.shape[0]
    R = EPILOGUE_ROWS
    PW = PROJ_PIECE
    r4 = lax.broadcasted_iota(jnp.int32, (R, PW), 0) & 3

    def gate_epilogue(r, z, j):
        rows = slice(r, r + R)
        cs = slice(j, j + PW)
        log2_lb, log2_1mlb, one_mlb, lb_floor = (lbc_ref[i:i + 1, cs] for i in range(4))
        z2 = z * LOG2E
        e = jnp.exp2(-jnp.abs(z2))
        inv = 1.0 / (1.0 + e)
        c = log2_1mlb + (jnp.minimum(z2, 0.0) - jnp.log2(1.0 + e))
        lf2 = jnp.maximum(log2_lb, c) + jnp.log2(1.0 + jnp.exp2(-jnp.abs(log2_lb - c)))
        lf2_hi = lf2.astype(BF16)
        pb_ref[rows, C_LH + j:C_LH + j + PW] = lf2_hi
        pb_ref[rows, C_LL + j:C_LL + j + PW] = (lf2 - lf2_hi.astype(F32)).astype(BF16)
        pos = z >= 0.0
        pb_ref[rows, C_K + j:C_K + j + PW] = (one_mlb * (jnp.where(pos, e, 1.0) * inv)).astype(BF16)
        f = lb_floor + one_mlb * (jnp.where(pos, 1.0, e) * inv)
        f_next = pltpu.roll(f, R - 1, 0)
        f_prev = pltpu.roll(f, 1, 0)
        pb_ref[rows, C_E1 + j:C_E1 + j + PW] = jnp.where((r4 & 1) == 1, f, 1.0).astype(BF16)
        e2 = jnp.where(r4 == 0, f_next, jnp.where(r4 == 1, 1.0, jnp.where(r4 == 2, f, f * f_prev)))
        pb_ref[rows, C_E2 + j:C_E2 + j + PW] = e2.astype(BF16)
        return e2

    def store(col, fn):
        def epilogue(r, x, j):
            y = fn(x)
            pb_ref[r:r + R, col + j:col + j + PW] = y.astype(BF16)
            return y
        return epilogue

    slabs = [
        (512, 512, gate_epilogue),
        (0, 512, store(C_Q, lambda x: x * _sigmoid(x))),
        (1536, 512, store(C_G, _sigmoid)),
        (1024, 512, store(C_V, lambda x: x)),
        (2048, 512, store(C_AQ, lambda x: x * (AT_HEAD_DIM ** -0.5 * LOG2E))),
        (2560, 256, store(C_AK, lambda x: x)),
    ]
    cols = [(base + j, j, epi) for base, width, epi in slabs for j in range(0, width, PW)]
    pieces = [(lo, j, epi, x_ref) for (lo, j, epi), x_ref in zip(cols, x_refs, strict=True)]

    zero = jnp.minimum(pl.program_id(0), 0)

    def tasks(piece):
        _, j, epi, x_ref = piece
        return [functools.partial(
            lambda r, j, epi, x_ref: epi(r, x_ref[pl.ds(pl.multiple_of(zero + r, R), R), :], j), r, j, epi, x_ref)
            for r in range(0, n, R)]

    n_gate = slabs[0][1] // PW
    heavy = [t for p in pieces[:n_gate] for t in tasks(p)]
    n_later = len(pieces) - n_gate
    share = -(-len(heavy) // n_later)
    for i, (lo, j, epi, x_ref) in enumerate(pieces):
        x_ref[...] = proj(lo, lo + PW)
        if i >= n_gate:
            for t in heavy[(i - n_gate) * share:(i - n_gate + 1) * share]:
                t()
        if i > n_gate:
            for t in tasks(pieces[i - 1]):
                t()
    for t in tasks(pieces[-1]):
        t()


def _layer_weight(layer, rows, cols):
    return pl.BlockSpec((None, rows, cols), lambda b, s: (layer, 0, 0), pipeline_mode=pl.Buffered(1))


def _inproj(h, g, w, lbc, ts, layer):
    B, S, _ = h.shape
    const = lambda b, s: (0, 0)
    return pl.pallas_call(
        _inproj_kernel,
        grid=(B, S // ts),
        in_specs=[
            pl.BlockSpec((None, ts, D_MODEL), lambda b, s: (b, s, 0)),
            pl.BlockSpec((1, D_MODEL), const),
            _layer_weight(layer, D_MODEL, IN_WIDTH),
            pl.BlockSpec((4, HG_WIDTH), const),
        ],
        out_specs=pl.BlockSpec((None, ts, PB_WIDTH), lambda b, s: (b, s, 0)),
        out_shape=jax.ShapeDtypeStruct((B, S, PB_WIDTH), BF16),
        scratch_shapes=[pltpu.VMEM((ts, PROJ_PIECE), F32)] * (IN_WIDTH // PROJ_PIECE),
        compiler_params=pltpu.CompilerParams(
            dimension_semantics=("arbitrary", "arbitrary"), vmem_limit_bytes=VMEM_LIMIT_BYTES),
        name="inproj",
    )(h, g, w, lbc)


HG_LEVELS = (32, 16, 8, 4, 2, 1)


def _row_blocks(x, split):
    lane = lax.broadcasted_iota(jnp.int32, x.shape, 1)
    zero = jnp.zeros_like(x)
    return jnp.concatenate([jnp.where(lane < split, x, zero), jnp.where(lane >= split, x, zero)], axis=0)


def _hgrn_consts():
    C = HG_CHUNK
    tt = lax.broadcasted_iota(jnp.int32, (C, C), 0)
    ss = lax.broadcasted_iota(jnp.int32, (C, C), 1)
    tril = jnp.where(tt >= ss, 1.0, 0.0).astype(BF16)
    level_masks = [((tt // (2 * m)) == (ss // (2 * m))) & ((tt & m) != 0) & ((ss & m) == 0) for m in HG_LEVELS]
    sub = lax.broadcasted_iota(jnp.int32, (C, HG_WIDTH), 0) & 7
    sign4 = jnp.where(sub >= 4, 1.0, -1.0)
    return tril, level_masks, tt == ss, sign4


def _hgrn_chunk(c, consts, pb_ref, b_ref, ng_ref, st_ref, y_ref):
    C = HG_CHUNK
    tril, level_masks, diag_mask, sign4 = consts
    rows = pl.ds(c * C, C)
    q = pb_ref[rows, C_Q:C_Q + 512]
    k = pb_ref[rows, C_K:C_K + 512]
    v = pb_ref[rows, C_V:C_V + 512]
    gs = pb_ref[rows, C_G:C_G + 512]

    b = b_ref[rows, :]
    b_last = b[C - 1:C, :]

    factors = []
    for m in (32, 16, 8):
        parts = []
        for g in range(C // (2 * m)):
            lo = 2 * m * g
            bp = jnp.broadcast_to(b[lo + m - 1:lo + m, :], (m, HG_WIDTH))
            parts += [bp - b[lo:lo + m, :], b[lo + m:lo + 2 * m, :] - bp]
        factors.append(jnp.exp2(jnp.concatenate(parts, axis=0)).astype(BF16))
    bp4 = jnp.concatenate([jnp.broadcast_to(b[8 * g + 3:8 * g + 4, :], (8, HG_WIDTH)) for g in range(C // 8)], axis=0)
    factors.append(jnp.exp2((b - bp4) * sign4).astype(BF16))
    factors.append(pb_ref[rows, C_E2:C_E2 + 512])
    factors.append(pb_ref[rows, C_E1:C_E1 + 512])

    qk = q * k
    att = [jnp.where(diag_mask, jnp.sum(qk[:, h * HG_DK:(h + 1) * HG_DK].astype(F32), axis=-1, keepdims=True), 0.0)
           for h in range(HG_HEADS)]
    for e, mask in zip(factors, level_masks):
        qt = q * e
        kt = k * e
        for h in range(HG_HEADS):
            sl = slice(h * HG_DK, (h + 1) * HG_DK)
            att[h] = jnp.where(mask, _dot_nt(qt[:, sl], kt[:, sl]), att[h])

    q_in = q * jnp.exp2(b).astype(BF16)
    k_out = k * jnp.exp2(b_last - b).astype(BF16)
    decay = jnp.exp2(b_last)
    for h in range(HG_HEADS):
        sl = slice(h * HG_DK, (h + 1) * HG_DK)
        st = st_ref[h]
        o = _dot(att[h].astype(BF16), v[:, sl]) + _dot(q_in[:, sl], st.T.astype(BF16))
        st_ref[h] = st * decay[:, sl] + _dot_tn(v[:, sl], k_out[:, sl])
        y_ref[rows, sl] = (_rms_scale(o) * ng_ref[:, sl]).astype(BF16) * gs[:, sl]


def _swa_consts():
    W = WINDOW
    qi = lax.broadcasted_iota(jnp.int32, (W, 4 * W), 0)
    kj = lax.broadcasted_iota(jnp.int32, (W, 4 * W), 1) & (2 * W - 1)
    rel = kj - qi
    band = (rel >= 1) & (rel <= W)
    lane = lax.broadcasted_iota(jnp.int32, (2 * W, W), 1)
    ones_rows = jnp.concatenate([jnp.where(lane < AT_HEAD_DIM, 1.0, 0.0),
                                 jnp.where(lane >= AT_HEAD_DIM, 1.0, 0.0)], axis=0).astype(BF16)
    return band, kj, ones_rows


def _swa_block(j, first_block, consts, sink_ref, pb_ref, kd_ref, vd_ref, y_ref):
    W = WINDOW
    D = AT_HEAD_DIM
    NP = AT_GROUP // 2
    band, kj, ones_rows = consts
    r0 = j * W
    valid = band if first_block is None else band & (kj >= jnp.where(first_block, W, 0))
    valid = jnp.concatenate([valid] * NP, axis=0)
    lane = lax.broadcasted_iota(jnp.int32, (NP * W, 2 * D), 1)
    upper = lax.broadcasted_iota(jnp.int32, (NP * W, 1), 0) >= W
    for hk in range(AT_KV_HEADS):
        kd = kd_ref[pl.ds(r0, 2 * W), hk * 2 * D:(hk + 1) * 2 * D]
        vd = vd_ref[pl.ds(r0, 2 * W), hk * 2 * D:(hk + 1) * 2 * D]
        k_blocks = _row_blocks(kd, D)
        v_blocks = jnp.concatenate([_row_blocks(vd, D), ones_rows], axis=1)
        c0 = hk * AT_GROUP * D
        q = jnp.concatenate([pb_ref[pl.ds(r0, W), C_AQ + c0 + 2 * D * pr:C_AQ + c0 + 2 * D * (pr + 1)]
                             for pr in range(NP)], axis=0)
        s = jnp.where(valid, _dot_nt(q, k_blocks), MASK_VALUE)
        sk = [jnp.where(upper, sink_ref[hk * AT_GROUP + 2 + i], sink_ref[hk * AT_GROUP + i]) * LOG2E
              for i in range(2)]
        m = [jnp.maximum(jnp.max(s[:, i * 2 * W:(i + 1) * 2 * W], axis=-1, keepdims=True), sk[i]) for i in range(2)]
        e = jnp.concatenate([jnp.exp2(s[:, i * 2 * W:(i + 1) * 2 * W] - m[i]) for i in range(2)], axis=1)
        o = _dot(e.astype(BF16), v_blocks)
        sink_term = jnp.where(lane < D, jnp.exp2(sk[0] - m[0]), jnp.exp2(sk[1] - m[1]))
        out = (o[:, 0:2 * D] / (o[:, 2 * D:4 * D] + sink_term)).astype(BF16)
        for pr in range(NP):
            y_ref[pl.ds(r0, W), c0 + 2 * D * pr:c0 + 2 * D * (pr + 1)] = out[pr * W:(pr + 1) * W, :]


def _dup_heads(x):
    lane = lax.broadcasted_iota(jnp.int32, x.shape, 1)
    swapped = pltpu.roll(x, AT_HEAD_DIM, 1)
    low = lane < AT_HEAD_DIM
    return jnp.concatenate([jnp.where(low, x, swapped), jnp.where(low, swapped, x)], axis=1)


def _mixer_kernel(sink_ref, pb_ref, kvprev_ref, h_ref, wout_ref, ng_ref, o_ref,
                  st_ref, yh_ref, ya_ref, kd_ref, vd_ref, b_ref, *, ts):
    si = pl.program_id(1)
    W = WINDOW

    @pl.when(si == 0)
    def _():
        st_ref[...] = jnp.zeros_like(st_ref)

    hgrn_consts = _hgrn_consts()
    tril = hgrn_consts[0]
    for c in range(ts // HG_CHUNK):
        rows = slice(c * HG_CHUNK, (c + 1) * HG_CHUNK)
        bw = _dot(tril, pb_ref[rows, C_LH:C_LH + 1024])
        b_ref[rows, :] = bw[:, 0:512] + bw[:, 512:1024]

    kd_ref[0:W, :] = _dup_heads(kvprev_ref[:, 0:KV_WIDTH].astype(F32)).astype(BF16)
    vd_ref[0:W, :] = _dup_heads(kvprev_ref[:, KV_WIDTH:2 * KV_WIDTH].astype(F32)).astype(BF16)
    kd_ref[W:W + ts, :] = _dup_heads(pb_ref[:, C_AK:C_AK + KV_WIDTH].astype(F32)).astype(BF16)
    vd_ref[W:W + ts, :] = _dup_heads(pb_ref[:, C_AK + KV_WIDTH:C_AK + 2 * KV_WIDTH].astype(F32)).astype(BF16)
    swa_consts = _swa_consts()

    for j in range(ts // W):
        first_block = (si == 0) if j == 0 else None
        _swa_block(j, first_block, swa_consts, sink_ref, pb_ref, kd_ref, vd_ref, ya_ref)
    acc = h_ref[...] + _dot(ya_ref[...], wout_ref[HG_WIDTH:D_MODEL, :])
    for c in range(ts // HG_CHUNK):
        _hgrn_chunk(c, hgrn_consts, pb_ref, b_ref, ng_ref, st_ref, yh_ref)
    o_ref[...] = acc + _dot(yh_ref[...], wout_ref[0:HG_WIDTH, :])


def _mixer(pb, h, w_out, ng, sinks, ts, layer):
    B, S, _ = h.shape
    nblk = ts // WINDOW
    const = lambda b, s: (0, 0)
    return pl.pallas_call(
        functools.partial(_mixer_kernel, ts=ts),
        grid=(B, S // ts),
        in_specs=[
            pl.BlockSpec(memory_space=pltpu.SMEM),
            pl.BlockSpec((None, ts, PB_WIDTH), lambda b, s: (b, s, 0)),
            pl.BlockSpec((None, WINDOW, 2 * KV_WIDTH),
                         lambda b, s: (b, jnp.maximum(s * nblk - 1, 0), C_AK // (2 * KV_WIDTH))),
            pl.BlockSpec((None, ts, D_MODEL), lambda b, s: (b, s, 0)),
            _layer_weight(layer, D_MODEL, D_MODEL),
            pl.BlockSpec((1, HG_WIDTH), const),
        ],
        out_specs=pl.BlockSpec((None, ts, D_MODEL), lambda b, s: (b, s, 0)),
        out_shape=jax.ShapeDtypeStruct((B, S, D_MODEL), F32),
        scratch_shapes=[
            pltpu.VMEM((HG_HEADS, HG_DK, HG_DK), F32),
            pltpu.VMEM((ts, HG_WIDTH), BF16),
            pltpu.VMEM((ts, AT_WIDTH), BF16),
            pltpu.VMEM((ts + WINDOW, 2 * KV_WIDTH), BF16),
            pltpu.VMEM((ts + WINDOW, 2 * KV_WIDTH), BF16),
            pltpu.VMEM((ts, HG_WIDTH), F32),
        ],
        compiler_params=pltpu.CompilerParams(
            dimension_semantics=("arbitrary", "arbitrary"), vmem_limit_bytes=VMEM_LIMIT_BYTES),
        name="mixer",
    )(sinks, pb, pb, h, w_out, ng)


def _ffn_kernel(h_ref, gf_ref, wup_ref, cw_ref, cb_ref, wdn_ref, gp_ref, wpg_ref, p_ref, wpu_ref,
                gfin_ref, o_ref, a_ref, gate_ref, *, ts, final_norm):
    si = pl.program_id(1)
    P = CONV_PAD

    @pl.when(si == 0)
    def _():
        a_ref[:, 0:P, :] = jnp.zeros((N_FF_CHUNKS, P, FF_CHUNK), F32)

    h = h_ref[...]
    u = (_rms_scale(h) * gf_ref[...]).astype(BF16)
    for c in range(N_FF_CHUNKS):
        cs = slice(c * FF_CHUNK, (c + 1) * FF_CHUNK)
        a = _dot(u, wup_ref[:, cs])
        b = _dot(u, wup_ref[:, D_FF + c * FF_CHUNK:D_FF + (c + 1) * FF_CHUNK])
        a_ref[c, P:P + ts, :] = a
        conv = (a * cw_ref[2:3, cs] + a_ref[c, P - 1:P - 1 + ts, :] * cw_ref[1:2, cs]
                + a_ref[c, P - 2:P - 2 + ts, :] * cw_ref[0:1, cs] + cb_ref[:, cs])
        gate_ref[:, cs] = (conv * _sigmoid(conv) * b).astype(BF16)
        a_ref[c, 0:P, :] = a_ref[c, ts:ts + P, :]
    h1 = h + _dot(gate_ref[...], wdn_ref[...])

    u2 = (_rms_scale(h1) * gp_ref[...]).astype(BF16)
    ple_gate = _sigmoid(_dot(u2, wpg_ref[...]))
    h2 = h1 + ple_gate * _dot(p_ref[...].astype(BF16), wpu_ref[...])
    if final_norm:
        h2 = _rms_scale(h2) * gfin_ref[...]
    o_ref[...] = h2


def _ffn(h, gf, w_up, cw, cb, w_dn, gp, wpg, p, wpu, gfin, ts, layer, final_norm):
    B, S, _ = h.shape
    const = lambda b, s: (0, 0)
    resident = functools.partial(_layer_weight, layer)
    return pl.pallas_call(
        functools.partial(_ffn_kernel, ts=ts, final_norm=final_norm),
        grid=(B, S // ts),
        in_specs=[
            pl.BlockSpec((None, ts, D_MODEL), lambda b, s: (b, s, 0)),
            pl.BlockSpec((1, D_MODEL), const),
            resident(D_MODEL, 2 * D_FF),
            pl.BlockSpec((3, D_FF), const),
            pl.BlockSpec((1, D_FF), const),
            resident(D_FF, D_MODEL),
            pl.BlockSpec((1, D_MODEL), const),
            resident(D_MODEL, D_MODEL),
            pl.BlockSpec((None, None, ts, PLE_DIM), lambda b, s: (layer, b, s, 0)),
            resident(PLE_DIM, D_MODEL),
            pl.BlockSpec((1, D_MODEL), const),
        ],
        out_specs=pl.BlockSpec((None, ts, D_MODEL), lambda b, s: (b, s, 0)),
        out_shape=jax.ShapeDtypeStruct((B, S, D_MODEL), F32),
        scratch_shapes=[
            pltpu.VMEM((N_FF_CHUNKS, CONV_PAD + ts, FF_CHUNK), F32),
            pltpu.VMEM((ts, D_FF), BF16),
        ],
        compiler_params=pltpu.CompilerParams(
            dimension_semantics=("arbitrary", "arbitrary"), vmem_limit_bytes=VMEM_LIMIT_BYTES),
        name="ffn_ple",
    )(h, gf, w_up, cw, cb, w_dn, gp, wpg, p, wpu, gfin)


def kernel(x, p, g_mix, w_in, lb_logits, hg_norm_g, attn_sinks, w_out, g_ffn, w_up, conv_w, conv_b,
           w_down, g_ple, w_ple_gate, w_ple_up, g_final):
    depth = w_in.shape[0]
    B, S, _ = x.shape
    ts = min(TOKEN_TILE, S)
    assert S % ts == 0 and ts % WINDOW == 0

    lb_p = jax.nn.softmax(lb_logits.astype(F32), axis=0)
    lb_all = jnp.cumsum(lb_p, axis=0) - lb_p[0]
    lb_floor = jnp.maximum(lb_all, LB_FLOOR)
    lbc = jnp.stack([jnp.log(lb_floor) * LOG2E, jnp.log1p(-lb_all) * LOG2E, 1.0 - lb_all, lb_floor], axis=1)

    w_in, w_out, w_up, w_down, w_ple_gate, w_ple_up = (
        w.astype(BF16) for w in (w_in, w_out, w_up, w_down, w_ple_gate, w_ple_up))
    row = lambda a: a.reshape(1, -1).astype(F32)
    h = x
    for i in range(depth):
        pb = _inproj(h, row(g_mix[i]), w_in, lbc[i], ts, i)
        h = _mixer(pb, h, w_out, row(hg_norm_g[i]), attn_sinks[i].astype(F32), ts, i)
        h = _ffn(h, row(g_ffn[i]), w_up, conv_w[i].astype(F32), row(conv_b[i]), w_down, row(g_ple[i]),
                 w_ple_gate, p, w_ple_up, row(g_final), ts, i, final_norm=(i == depth - 1))
    return h
```

```python
import functools

import jax
import jax.numpy as jnp
from jax import lax
from jax.experimental import pallas as pl
from jax.experimental.pallas import tpu as pltpu

F32 = jnp.float32
BF16 = jnp.bfloat16

D_MODEL = 1024
PLE_DIM = 256
HG_WIDTH = 512
HG_HEADS = 4
HG_DK = 128
HG_CHUNK = 64
AT_WIDTH = 512
AT_HEAD_DIM = 64
AT_Q_HEADS = 8
AT_KV_HEADS = 2
AT_GROUP = 4
WINDOW = 128
KV_WIDTH = AT_KV_HEADS * AT_HEAD_DIM
D_FF = 2816
EPS = 1e-6
MASK_VALUE = -1e30
LB_FLOOR = 1e-30
IN_WIDTH = 4 * HG_WIDTH + AT_WIDTH + 2 * KV_WIDTH

C_Q, C_K, C_V, C_G, C_AQ, C_AK, C_LH, C_LL, C_E1, C_E2 = 0, 512, 1024, 1536, 2048, 2560, 2816, 3328, 3840, 4352
PB_WIDTH = C_E2 + HG_WIDTH
LOG2E = 1.4426950408889634

FF_CHUNK = 256
N_FF_CHUNKS = D_FF // FF_CHUNK
CONV_PAD = 8

VMEM_LIMIT_BYTES = 56 * 1024 * 1024

TOKEN_TILE = 1024
INPROJ_TILE = 512
MIXER_TILE = 1024
EPILOGUE_ROWS = 16
PROJ_PIECE = 256


def _dot(a, b):
    return jnp.dot(a, b, preferred_element_type=F32)


def _dot_nt(a, b):
    return lax.dot_general(a, b, (((1,), (1,)), ((), ())), preferred_element_type=F32)


def _dot_tn(a, b):
    return lax.dot_general(a, b, (((0,), (0,)), ((), ())), preferred_element_type=F32)


def _sigmoid(x):
    return 1.0 / (1.0 + jnp.exp(-x))


def _rms_scale(x):
    return x * lax.rsqrt(jnp.mean(x * x, axis=-1, keepdims=True) + EPS)


def _inproj_kernel(h_ref, g_ref, w_ref, lbc_ref, pb_ref, *x_refs):
    u = (_rms_scale(h_ref[...]) * g_ref[...]).astype(BF16)

    def proj(lo, hi):
        return _dot(u, w_ref[:, lo:hi])

    n = u.shape[0]
    R = EPILOGUE_ROWS
    PW = PROJ_PIECE
    r4 = lax.broadcasted_iota(jnp.int32, (R, PW), 0) & 3

    def gate_epilogue(r, z, j):
        rows = slice(r, r + R)
        cs = slice(j, j + PW)
        log2_lb, log2_1mlb, one_mlb, lb_floor = (lbc_ref[i:i + 1, cs] for i in range(4))
        z2 = z * LOG2E
        e = jnp.exp2(-jnp.abs(z2))
        inv = 1.0 / (1.0 + e)
        c = log2_1mlb + (jnp.minimum(z2, 0.0) - jnp.log2(1.0 + e))
        lf2 = jnp.maximum(log2_lb, c) + jnp.log2(1.0 + jnp.exp2(-jnp.abs(log2_lb - c)))
        lf2_hi = lf2.astype(BF16)
        pb_ref[rows, C_LH + j:C_LH + j + PW] = lf2_hi
        pb_ref[rows, C_LL + j:C_LL + j + PW] = (lf2 - lf2_hi.astype(F32)).astype(BF16)
        pos = z >= 0.0
        pb_ref[rows, C_K + j:C_K + j + PW] = (one_mlb * (jnp.where(pos, e, 1.0) * inv)).astype(BF16)
        f = lb_floor + one_mlb * (jnp.where(pos, 1.0, e) * inv)
        f_next = pltpu.roll(f, R - 1, 0)
        f_prev = pltpu.roll(f, 1, 0)
        pb_ref[rows, C_E1 + j:C_E1 + j + PW] = jnp.where((r4 & 1) == 1, f, 1.0).astype(BF16)
        e2 = jnp.where(r4 == 0, f_next, jnp.where(r4 == 1, 1.0, jnp.where(r4 == 2, f, f * f_prev)))
        pb_ref[rows, C_E2 + j:C_E2 + j + PW] = e2.astype(BF16)
        return e2

    def store(col, fn):
        def epilogue(r, x, j):
            y = fn(x)
            pb_ref[r:r + R, col + j:col + j + PW] = y.astype(BF16)
            return y
        return epilogue

    slabs = [
        (512, 512, gate_epilogue),
        (0, 512, store(C_Q, lambda x: x * _sigmoid(x))),
        (1536, 512, store(C_G, _sigmoid)),
        (1024, 512, store(C_V, lambda x: x)),
        (2048, 512, store(C_AQ, lambda x: x * (AT_HEAD_DIM ** -0.5 * LOG2E))),
        (2560, 256, store(C_AK, lambda x: x)),
    ]
    cols = [(base + j, j, epi) for base, width, epi in slabs for j in range(0, width, PW)]
    pieces = [(lo, j, epi, x_ref) for (lo, j, epi), x_ref in zip(cols, x_refs, strict=True)]

    zero = jnp.minimum(pl.program_id(0), 0)

    def tasks(piece):
        _, j, epi, x_ref = piece
        return [functools.partial(
            lambda r, j, epi, x_ref: epi(r, x_ref[pl.ds(pl.multiple_of(zero + r, R), R), :], j), r, j, epi, x_ref)
            for r in range(0, n, R)]

    n_gate = slabs[0][1] // PW
    heavy = [t for p in pieces[:n_gate] for t in tasks(p)]
    n_later = len(pieces) - n_gate
    share = -(-len(heavy) // n_later)
    for i, (lo, j, epi, x_ref) in enumerate(pieces):
        x_ref[...] = proj(lo, lo + PW)
        if i >= n_gate:
            for t in heavy[(i - n_gate) * share:(i - n_gate + 1) * share]:
                t()
        if i > n_gate:
            for t in tasks(pieces[i - 1]):
                t()
    for t in tasks(pieces[-1]):
        t()


def _layer_weight(layer, rows, cols):
    return pl.BlockSpec((None, rows, cols), lambda b, s: (layer, 0, 0), pipeline_mode=pl.Buffered(1))


def _inproj(h, g, w, lbc, ts, layer):
    B, S, _ = h.shape
    const = lambda b, s: (0, 0)
    return pl.pallas_call(
        _inproj_kernel,
        grid=(B, S // ts),
        in_specs=[
            pl.BlockSpec((None, ts, D_MODEL), lambda b, s: (b, s, 0)),
            pl.BlockSpec((1, D_MODEL), const),
            _layer_weight(layer, D_MODEL, IN_WIDTH),
            pl.BlockSpec((4, HG_WIDTH), const),
        ],
        out_specs=pl.BlockSpec((None, ts, PB_WIDTH), lambda b, s: (b, s, 0)),
        out_shape=jax.ShapeDtypeStruct((B, S, PB_WIDTH), BF16),
        scratch_shapes=[pltpu.VMEM((ts, PROJ_PIECE), F32)] * (IN_WIDTH // PROJ_PIECE),
        compiler_params=pltpu.CompilerParams(
            dimension_semantics=("arbitrary", "arbitrary"), vmem_limit_bytes=VMEM_LIMIT_BYTES),
        name="inproj",
    )(h, g, w, lbc)


HG_LEVELS = (32, 16, 8, 4, 2, 1)


def _row_blocks(x, split):
    lane = lax.broadcasted_iota(jnp.int32, x.shape, 1)
    zero = jnp.zeros_like(x)
    return jnp.concatenate([jnp.where(lane < split, x, zero), jnp.where(lane >= split, x, zero)], axis=0)


def _hgrn_consts():
    C = HG_CHUNK
    tt = lax.broadcasted_iota(jnp.int32, (C, C), 0)
    ss = lax.broadcasted_iota(jnp.int32, (C, C), 1)
    tril = jnp.where(tt >= ss, 1.0, 0.0).astype(BF16)
    level_masks = [((tt // (2 * m)) == (ss // (2 * m))) & ((tt & m) != 0) & ((ss & m) == 0) for m in HG_LEVELS]
    sub = lax.broadcasted_iota(jnp.int32, (C, HG_WIDTH), 0) & 7
    sign4 = jnp.where(sub >= 4, 1.0, -1.0)
    return tril, level_masks, tt == ss, sign4


def _hgrn_chunk(c, consts, pb_ref, b_ref, ng_ref, st_ref, y_ref):
    C = HG_CHUNK
    tril, level_masks, diag_mask, sign4 = consts
    rows = pl.ds(c * C, C)
    q = pb_ref[rows, C_Q:C_Q + 512]
    k = pb_ref[rows, C_K:C_K + 512]
    v = pb_ref[rows, C_V:C_V + 512]
    gs = pb_ref[rows, C_G:C_G + 512]

    b = b_ref[rows, :]
    b_last = b[C - 1:C, :]

    factors = []
    for m in (32, 16, 8):
        parts = []
        for g in range(C // (2 * m)):
            lo = 2 * m * g
            bp = jnp.broadcast_to(b[lo + m - 1:lo + m, :], (m, HG_WIDTH))
            parts += [bp - b[lo:lo + m, :], b[lo + m:lo + 2 * m, :] - bp]
        factors.append(jnp.exp2(jnp.concatenate(parts, axis=0)).astype(BF16))
    bp4 = jnp.concatenate([jnp.broadcast_to(b[8 * g + 3:8 * g + 4, :], (8, HG_WIDTH)) for g in range(C // 8)], axis=0)
    factors.append(jnp.exp2((b - bp4) * sign4).astype(BF16))
    factors.append(pb_ref[rows, C_E2:C_E2 + 512])
    factors.append(pb_ref[rows, C_E1:C_E1 + 512])

    qk = q * k
    att = [jnp.where(diag_mask, jnp.sum(qk[:, h * HG_DK:(h + 1) * HG_DK].astype(F32), axis=-1, keepdims=True), 0.0)
           for h in range(HG_HEADS)]
    for e, mask in zip(factors, level_masks):
        qt = q * e
        kt = k * e
        for h in range(HG_HEADS):
            sl = slice(h * HG_DK, (h + 1) * HG_DK)
            att[h] = jnp.where(mask, _dot_nt(qt[:, sl], kt[:, sl]), att[h])

    q_in = q * jnp.exp2(b).astype(BF16)
    k_out = k * jnp.exp2(b_last - b).astype(BF16)
    decay = jnp.exp2(b_last)
    for h in range(HG_HEADS):
        sl = slice(h * HG_DK, (h + 1) * HG_DK)
        st = st_ref[h]
        o = _dot(att[h].astype(BF16), v[:, sl]) + _dot(q_in[:, sl], st.T.astype(BF16))
        st_ref[h] = st * decay[:, sl] + _dot_tn(v[:, sl], k_out[:, sl])
        y_ref[rows, sl] = (_rms_scale(o) * ng_ref[:, sl]).astype(BF16) * gs[:, sl]


def _swa_consts():
    W = WINDOW
    qi = lax.broadcasted_iota(jnp.int32, (W, 4 * W), 0)
    kj = lax.broadcasted_iota(jnp.int32, (W, 4 * W), 1) & (2 * W - 1)
    rel = kj - qi
    band = (rel >= 1) & (rel <= W)
    lane = lax.broadcasted_iota(jnp.int32, (2 * W, W), 1)
    ones_rows = jnp.concatenate([jnp.where(lane < AT_HEAD_DIM, 1.0, 0.0),
                                 jnp.where(lane >= AT_HEAD_DIM, 1.0, 0.0)], axis=0).astype(BF16)
    return band, kj, ones_rows


def _swa_block(j, first_block, consts, sink_ref, pb_ref, kd_ref, vd_ref, y_ref):
    W = WINDOW
    D = AT_HEAD_DIM
    NP = AT_GROUP // 2
    band, kj, ones_rows = consts
    r0 = j * W
    valid = band if first_block is None else band & (kj >= jnp.where(first_block, W, 0))
    valid = jnp.concatenate([valid] * NP, axis=0)
    lane = lax.broadcasted_iota(jnp.int32, (NP * W, 2 * D), 1)
    upper = lax.broadcasted_iota(jnp.int32, (NP * W, 1), 0) >= W
    for hk in range(AT_KV_HEADS):
        kd = kd_ref[pl.ds(r0, 2 * W), hk * 2 * D:(hk + 1) * 2 * D]
        vd = vd_ref[pl.ds(r0, 2 * W), hk * 2 * D:(hk + 1) * 2 * D]
        k_blocks = _row_blocks(kd, D)
        v_blocks = jnp.concatenate([_row_blocks(vd, D), ones_rows], axis=1)
        c0 = hk * AT_GROUP * D
        q = jnp.concatenate([pb_ref[pl.ds(r0, W), C_AQ + c0 + 2 * D * pr:C_AQ + c0 + 2 * D * (pr + 1)]
                             for pr in range(NP)], axis=0)
        s = jnp.where(valid, _dot_nt(q, k_blocks), MASK_VALUE)
        sk = [jnp.where(upper, sink_ref[hk * AT_GROUP + 2 + i], sink_ref[hk * AT_GROUP + i]) * LOG2E
              for i in range(2)]
        m = [jnp.maximum(jnp.max(s[:, i * 2 * W:(i + 1) * 2 * W], axis=-1, keepdims=True), sk[i]) for i in range(2)]
        e = jnp.concatenate([jnp.exp2(s[:, i * 2 * W:(i + 1) * 2 * W] - m[i]) for i in range(2)], axis=1)
        o = _dot(e.astype(BF16), v_blocks)
        sink_term = jnp.where(lane < D, jnp.exp2(sk[0] - m[0]), jnp.exp2(sk[1] - m[1]))
        out = (o[:, 0:2 * D] / (o[:, 2 * D:4 * D] + sink_term)).astype(BF16)
        for pr in range(NP):
            y_ref[pl.ds(r0, W), c0 + 2 * D * pr:c0 + 2 * D * (pr + 1)] = out[pr * W:(pr + 1) * W, :]


def _dup_heads(x):
    lane = lax.broadcasted_iota(jnp.int32, x.shape, 1)
    swapped = pltpu.roll(x, AT_HEAD_DIM, 1)
    low = lane < AT_HEAD_DIM
    return jnp.concatenate([jnp.where(low, x, swapped), jnp.where(low, swapped, x)], axis=1)


def _mixer_kernel(sink_ref, pb_ref, kvprev_ref, h_ref, wout_ref, ng_ref, o_ref,
                  st_ref, yh_ref, ya_ref, kd_ref, vd_ref, b_ref, *, ts):
    si = pl.program_id(1)
    W = WINDOW

    @pl.when(si == 0)
    def _():
        st_ref[...] = jnp.zeros_like(st_ref)

    hgrn_consts = _hgrn_consts()
    tril = hgrn_consts[0]
    for c in range(ts // HG_CHUNK):
        rows = slice(c * HG_CHUNK, (c + 1) * HG_CHUNK)
        bw = _dot(tril, pb_ref[rows, C_LH:C_LH + 1024])
        b_ref[rows, :] = bw[:, 0:512] + bw[:, 512:1024]

    kd_ref[0:W, :] = _dup_heads(kvprev_ref[:, 0:KV_WIDTH].astype(F32)).astype(BF16)
    vd_ref[0:W, :] = _dup_heads(kvprev_ref[:, KV_WIDTH:2 * KV_WIDTH].astype(F32)).astype(BF16)
    kd_ref[W:W + ts, :] = _dup_heads(pb_ref[:, C_AK:C_AK + KV_WIDTH].astype(F32)).astype(BF16)
    vd_ref[W:W + ts, :] = _dup_heads(pb_ref[:, C_AK + KV_WIDTH:C_AK + 2 * KV_WIDTH].astype(F32)).astype(BF16)
    swa_consts = _swa_consts()

    for j in range(ts // W):
        first_block = (si == 0) if j == 0 else None
        _swa_block(j, first_block, swa_consts, sink_ref, pb_ref, kd_ref, vd_ref, ya_ref)
    acc = h_ref[...] + _dot(ya_ref[...], wout_ref[HG_WIDTH:D_MODEL, :])
    for c in range(ts // HG_CHUNK):
        _hgrn_chunk(c, hgrn_consts, pb_ref, b_ref, ng_ref, st_ref, yh_ref)
    o_ref[...] = acc + _dot(yh_ref[...], wout_ref[0:HG_WIDTH, :])


def _mixer(pb, h, w_out, ng, sinks, ts, layer):
    B, S, _ = h.shape
    nblk = ts // WINDOW
    const = lambda b, s: (0, 0)
    return pl.pallas_call(
        functools.partial(_mixer_kernel, ts=ts),
        grid=(B, S // ts),
        in_specs=[
            pl.BlockSpec(memory_space=pltpu.SMEM),
            pl.BlockSpec((None, ts, PB_WIDTH), lambda b, s: (b, s, 0)),
            pl.BlockSpec((None, WINDOW, 2 * KV_WIDTH),
                         lambda b, s: (b, jnp.maximum(s * nblk - 1, 0), C_AK // (2 * KV_WIDTH))),
            pl.BlockSpec((None, ts, D_MODEL), lambda b, s: (b, s, 0)),
            _layer_weight(layer, D_MODEL, D_MODEL),
            pl.BlockSpec((1, HG_WIDTH), const),
        ],
        out_specs=pl.BlockSpec((None, ts, D_MODEL), lambda b, s: (b, s, 0)),
        out_shape=jax.ShapeDtypeStruct((B, S, D_MODEL), F32),
        scratch_shapes=[
            pltpu.VMEM((HG_HEADS, HG_DK, HG_DK), F32),
            pltpu.VMEM((ts, HG_WIDTH), BF16),
            pltpu.VMEM((ts, AT_WIDTH), BF16),
            pltpu.VMEM((ts + WINDOW, 2 * KV_WIDTH), BF16),
            pltpu.VMEM((ts + WINDOW, 2 * KV_WIDTH), BF16),
            pltpu.VMEM((ts, HG_WIDTH), F32),
        ],
        compiler_params=pltpu.CompilerParams(
            dimension_semantics=("arbitrary", "arbitrary"), vmem_limit_bytes=VMEM_LIMIT_BYTES),
        name="mixer",
    )(sinks, pb, pb, h, w_out, ng)


def _ffn_kernel(h_ref, gf_ref, wup_ref, cw_ref, cb_ref, wdn_ref, gp_ref, wpg_ref, p_ref, wpu_ref,
                gfin_ref, o_ref, a_ref, hist_ref, gate_ref, *, ts, final_norm):
    si = pl.program_id(1)
    P = CONV_PAD

    @pl.when(si == 0)
    def _():
        hist_ref[...] = jnp.zeros_like(hist_ref)

    h = h_ref[...]
    u = (_rms_scale(h) * gf_ref[...]).astype(BF16)
    for c in range(N_FF_CHUNKS):
        cs = slice(c * FF_CHUNK, (c + 1) * FF_CHUNK)
        a = _dot(u, wup_ref[:, cs])
        b = _dot(u, wup_ref[:, D_FF + c * FF_CHUNK:D_FF + (c + 1) * FF_CHUNK])
        w = c % 2
        a_ref[w, 0:P, :] = hist_ref[c]
        a_ref[w, P:P + ts, :] = a
        conv = (a * cw_ref[2:3, cs] + a_ref[w, P - 1:P - 1 + ts, :] * cw_ref[1:2, cs]
                + a_ref[w, P - 2:P - 2 + ts, :] * cw_ref[0:1, cs] + cb_ref[:, cs])
        gate_ref[:, cs] = (conv * _sigmoid(conv) * b).astype(BF16)
        hist_ref[c] = a_ref[w, ts:ts + P, :]
    h1 = h + _dot(gate_ref[...], wdn_ref[...])

    u2 = (_rms_scale(h1) * gp_ref[...]).astype(BF16)
    ple_gate = _sigmoid(_dot(u2, wpg_ref[...]))
    h2 = h1 + ple_gate * _dot(p_ref[...].astype(BF16), wpu_ref[...])
    if final_norm:
        h2 = _rms_scale(h2) * gfin_ref[...]
    o_ref[...] = h2


def _ffn(h, gf, w_up, cw, cb, w_dn, gp, wpg, p, wpu, gfin, ts, layer, final_norm):
    B, S, _ = h.shape
    const = lambda b, s: (0, 0)
    resident = functools.partial(_layer_weight, layer)
    return pl.pallas_call(
        functools.partial(_ffn_kernel, ts=ts, final_norm=final_norm),
        grid=(B, S // ts),
        in_specs=[
            pl.BlockSpec((None, ts, D_MODEL), lambda b, s: (b, s, 0)),
            pl.BlockSpec((1, D_MODEL), const),
            resident(D_MODEL, 2 * D_FF),
            pl.BlockSpec((3, D_FF), const),
            pl.BlockSpec((1, D_FF), const),
            resident(D_FF, D_MODEL),
            pl.BlockSpec((1, D_MODEL), const),
            resident(D_MODEL, D_MODEL),
            pl.BlockSpec((None, None, ts, PLE_DIM), lambda b, s: (layer, b, s, 0)),
            resident(PLE_DIM, D_MODEL),
            pl.BlockSpec((1, D_MODEL), const),
        ],
        out_specs=pl.BlockSpec((None, ts, D_MODEL), lambda b, s: (b, s, 0)),
        out_shape=jax.ShapeDtypeStruct((B, S, D_MODEL), F32),
        scratch_shapes=[
            pltpu.VMEM((2, CONV_PAD + ts, FF_CHUNK), F32),
            pltpu.VMEM((N_FF_CHUNKS, CONV_PAD, FF_CHUNK), F32),
            pltpu.VMEM((ts, D_FF), BF16),
        ],
        compiler_params=pltpu.CompilerParams(
            dimension_semantics=("arbitrary", "arbitrary"), vmem_limit_bytes=VMEM_LIMIT_BYTES),
        name="ffn_ple",
    )(h, gf, w_up, cw, cb, w_dn, gp, wpg, p, wpu, gfin)


def kernel(x, p, g_mix, w_in, lb_logits, hg_norm_g, attn_sinks, w_out, g_ffn, w_up, conv_w, conv_b,
           w_down, g_ple, w_ple_gate, w_ple_up, g_final):
    depth = w_in.shape[0]
    B, S, _ = x.shape
    ts_in, ts_mix, ts = (min(t, S) for t in (INPROJ_TILE, MIXER_TILE, TOKEN_TILE))
    assert all(S % t == 0 and t % WINDOW == 0 for t in (ts_in, ts_mix, ts))

    lb_p = jax.nn.softmax(lb_logits.astype(F32), axis=0)
    lb_all = jnp.cumsum(lb_p, axis=0) - lb_p[0]
    lb_floor = jnp.maximum(lb_all, LB_FLOOR)
    lbc = jnp.stack([jnp.log(lb_floor) * LOG2E, jnp.log1p(-lb_all) * LOG2E, 1.0 - lb_all, lb_floor], axis=1)

    w_in, w_out, w_up, w_down, w_ple_gate, w_ple_up = (
        w.astype(BF16) for w in (w_in, w_out, w_up, w_down, w_ple_gate, w_ple_up))
    row = lambda a: a.reshape(1, -1).astype(F32)
    h = x
    for i in range(depth):
        pb = _inproj(h, row(g_mix[i]), w_in, lbc[i], ts_in, i)
        h = _mixer(pb, h, w_out, row(hg_norm_g[i]), attn_sinks[i].astype(F32), ts_mix, i)
        h = _ffn(h, row(g_ffn[i]), w_up, conv_w[i].astype(F32), row(conv_b[i]), w_down, row(g_ple[i]),
                 w_ple_gate, p, w_ple_up, row(g_final), ts, i, final_norm=(i == depth - 1))
    return h
```

```python
import functools

import jax
import jax.numpy as jnp
from jax import lax
from jax.experimental import pallas as pl
from jax.experimental.pallas import tpu as pltpu

F32 = jnp.float32
BF16 = jnp.bfloat16

D_MODEL = 1024
PLE_DIM = 256
HG_WIDTH = 512
HG_HEADS = 4
HG_DK = 128
HG_CHUNK = 64
AT_WIDTH = 512
AT_HEAD_DIM = 64
AT_Q_HEADS = 8
AT_KV_HEADS = 2
AT_GROUP = 4
WINDOW = 128
KV_WIDTH = AT_KV_HEADS * AT_HEAD_DIM
D_FF = 2816
EPS = 1e-6
MASK_VALUE = -1e30
LB_FLOOR = 1e-30
IN_WIDTH = 4 * HG_WIDTH + AT_WIDTH + 2 * KV_WIDTH

C_Q, C_K, C_V, C_G, C_AQ, C_AK, C_LH, C_LL, C_E1, C_E2 = 0, 512, 1024, 1536, 2048, 2560, 2816, 3328, 3840, 4352
PB_WIDTH = C_E2 + HG_WIDTH
LOG2E = 1.4426950408889634

FF_CHUNK = 256
N_FF_CHUNKS = D_FF // FF_CHUNK
CONV_PAD = 8

VMEM_LIMIT_BYTES = 56 * 1024 * 1024

TOKEN_TILE = 1024
INPROJ_TILE = 512
MIXER_TILE = 1024
EPILOGUE_ROWS = 16
PROJ_PIECE = 256


def _dot(a, b):
    return jnp.dot(a, b, preferred_element_type=F32)


def _dot_nt(a, b):
    return lax.dot_general(a, b, (((1,), (1,)), ((), ())), preferred_element_type=F32)


def _dot_tn(a, b):
    return lax.dot_general(a, b, (((0,), (0,)), ((), ())), preferred_element_type=F32)


def _sigmoid(x):
    return 1.0 / (1.0 + jnp.exp(-x))


def _rms_scale(x):
    return x * lax.rsqrt(jnp.mean(x * x, axis=-1, keepdims=True) + EPS)


def _inproj_kernel(h_ref, g_ref, w_ref, lbc_ref, pb_ref, *x_refs):
    u = (_rms_scale(h_ref[...]) * g_ref[...]).astype(BF16)

    def proj(lo, hi):
        return _dot(u, w_ref[:, lo:hi])

    n = u.shape[0]
    R = EPILOGUE_ROWS
    PW = PROJ_PIECE
    r4 = lax.broadcasted_iota(jnp.int32, (R, PW), 0) & 3

    def gate_epilogue(r, z, j):
        rows = slice(r, r + R)
        cs = slice(j, j + PW)
        log2_lb, log2_1mlb, one_mlb, lb_floor = (lbc_ref[i:i + 1, cs] for i in range(4))
        z2 = z * LOG2E
        e = jnp.exp2(-jnp.abs(z2))
        inv = 1.0 / (1.0 + e)
        c = log2_1mlb + (jnp.minimum(z2, 0.0) - jnp.log2(1.0 + e))
        lf2 = jnp.maximum(log2_lb, c) + jnp.log2(1.0 + jnp.exp2(-jnp.abs(log2_lb - c)))
        lf2_hi = lf2.astype(BF16)
        pb_ref[rows, C_LH + j:C_LH + j + PW] = lf2_hi
        pb_ref[rows, C_LL + j:C_LL + j + PW] = (lf2 - lf2_hi.astype(F32)).astype(BF16)
        pos = z >= 0.0
        pb_ref[rows, C_K + j:C_K + j + PW] = (one_mlb * (jnp.where(pos, e, 1.0) * inv)).astype(BF16)
        f = lb_floor + one_mlb * (jnp.where(pos, 1.0, e) * inv)
        f_next = pltpu.roll(f, R - 1, 0)
        f_prev = pltpu.roll(f, 1, 0)
        pb_ref[rows, C_E1 + j:C_E1 + j + PW] = jnp.where((r4 & 1) == 1, f, 1.0).astype(BF16)
        e2 = jnp.where(r4 == 0, f_next, jnp.where(r4 == 1, 1.0, jnp.where(r4 == 2, f, f * f_prev)))
        pb_ref[rows, C_E2 + j:C_E2 + j + PW] = e2.astype(BF16)
        return e2

    def store(col, fn):
        def epilogue(r, x, j):
            y = fn(x)
            pb_ref[r:r + R, col + j:col + j + PW] = y.astype(BF16)
            return y
        return epilogue

    slabs = [
        (512, 512, gate_epilogue),
        (0, 512, store(C_Q, lambda x: x * _sigmoid(x))),
        (1536, 512, store(C_G, _sigmoid)),
        (1024, 512, store(C_V, lambda x: x)),
        (2048, 512, store(C_AQ, lambda x: x * (AT_HEAD_DIM ** -0.5 * LOG2E))),
        (2560, 256, store(C_AK, lambda x: x)),
    ]
    cols = [(base + j, j, epi) for base, width, epi in slabs for j in range(0, width, PW)]
    pieces = [(lo, j, epi, x_ref) for (lo, j, epi), x_ref in zip(cols, x_refs, strict=True)]

    zero = jnp.minimum(pl.program_id(0), 0)

    def tasks(piece):
        _, j, epi, x_ref = piece
        return [functools.partial(
            lambda r, j, epi, x_ref: epi(r, x_ref[pl.ds(pl.multiple_of(zero + r, R), R), :], j), r, j, epi, x_ref)
            for r in range(0, n, R)]

    n_gate = slabs[0][1] // PW
    heavy = [t for p in pieces[:n_gate] for t in tasks(p)]
    n_later = len(pieces) - n_gate
    share = -(-len(heavy) // n_later)
    for i, (lo, j, epi, x_ref) in enumerate(pieces):
        x_ref[...] = proj(lo, lo + PW)
        if i >= n_gate:
            for t in heavy[(i - n_gate) * share:(i - n_gate + 1) * share]:
                t()
        if i > n_gate:
            for t in tasks(pieces[i - 1]):
                t()
    for t in tasks(pieces[-1]):
        t()


def _layer_weight(layer, rows, cols):
    return pl.BlockSpec((None, rows, cols), lambda b, s: (layer, 0, 0), pipeline_mode=pl.Buffered(1))


def _inproj(h, g, w, lbc, ts, layer):
    B, S, _ = h.shape
    const = lambda b, s: (0, 0)
    return pl.pallas_call(
        _inproj_kernel,
        grid=(B, S // ts),
        in_specs=[
            pl.BlockSpec((None, ts, D_MODEL), lambda b, s: (b, s, 0)),
            pl.BlockSpec((1, D_MODEL), const),
            _layer_weight(layer, D_MODEL, IN_WIDTH),
            pl.BlockSpec((4, HG_WIDTH), const),
        ],
        out_specs=pl.BlockSpec((None, ts, PB_WIDTH), lambda b, s: (b, s, 0)),
        out_shape=jax.ShapeDtypeStruct((B, S, PB_WIDTH), BF16),
        scratch_shapes=[pltpu.VMEM((ts, PROJ_PIECE), F32)] * (IN_WIDTH // PROJ_PIECE),
        compiler_params=pltpu.CompilerParams(
            dimension_semantics=("arbitrary", "arbitrary"), vmem_limit_bytes=VMEM_LIMIT_BYTES),
        name="inproj",
    )(h, g, w, lbc)


HG_LEVELS = (32, 16, 8, 4, 2, 1)
HG_SINGLE_PIVOT_MAX_LOG2_DECAY = 80.0


def _row_blocks(x, split):
    lane = lax.broadcasted_iota(jnp.int32, x.shape, 1)
    zero = jnp.zeros_like(x)
    return jnp.concatenate([jnp.where(lane < split, x, zero), jnp.where(lane >= split, x, zero)], axis=0)


def _hgrn_consts():
    C = HG_CHUNK
    tt = lax.broadcasted_iota(jnp.int32, (C, C), 0)
    ss = lax.broadcasted_iota(jnp.int32, (C, C), 1)
    tril = jnp.where(tt >= ss, 1.0, 0.0).astype(BF16)
    level_masks = [((tt // (2 * m)) == (ss // (2 * m))) & ((tt & m) != 0) & ((ss & m) == 0) for m in HG_LEVELS]
    sub = lax.broadcasted_iota(jnp.int32, (C, HG_WIDTH), 0) & 7
    sign4 = jnp.where(sub >= 4, 1.0, -1.0)
    half_causal = (tt >= ss) & ((tt // (C // 2)) == (ss // (C // 2)))
    return tril, level_masks, tt == ss, sign4, half_causal


def _hgrn_chunk(c, consts, pb_ref, b_ref, ng_ref, st_ref, y_ref, single_pivot):
    C = HG_CHUNK
    tril, level_masks, diag_mask, sign4, half_causal_mask = consts
    rows = pl.ds(c * C, C)
    q = pb_ref[rows, C_Q:C_Q + 512]
    k = pb_ref[rows, C_K:C_K + 512]
    v = pb_ref[rows, C_V:C_V + 512]
    gs = pb_ref[rows, C_G:C_G + 512]

    b = b_ref[rows, :]
    b_last = b[C - 1:C, :]
    if single_pivot:
        H = C // 2
        bp = jnp.concatenate([jnp.broadcast_to(b[g * H + H // 2 - 1:g * H + H // 2, :], (H, HG_WIDTH))
                              for g in range(2)], axis=0)
        d = b - bp
        bm = jnp.broadcast_to(b[H - 1:H, :], (H, HG_WIDTH))
        e_cross = jnp.exp2(jnp.concatenate([bm - b[0:H, :], b[H:C, :] - bm], axis=0)).astype(BF16)
        qt, kt = q * jnp.exp2(d).astype(BF16), k * jnp.exp2(-d).astype(BF16)
        qc, kc = q * e_cross, k * e_cross
        att = []
        for h in range(HG_HEADS):
            sl = slice(h * HG_DK, (h + 1) * HG_DK)
            att.append(jnp.where(level_masks[0], _dot_nt(qc[:, sl], kc[:, sl]),
                                 jnp.where(half_causal_mask, _dot_nt(qt[:, sl], kt[:, sl]), 0.0)))
    else:
        att = _hgrn_halving_scores(c, consts, pb_ref, q, k, b)

    q_in = q * jnp.exp2(b).astype(BF16)
    k_out = k * jnp.exp2(b_last - b).astype(BF16)
    decay = jnp.exp2(b_last)
    for h in range(HG_HEADS):
        sl = slice(h * HG_DK, (h + 1) * HG_DK)
        st = st_ref[h]
        o = _dot(att[h].astype(BF16), v[:, sl]) + _dot(q_in[:, sl], st.T.astype(BF16))
        st_ref[h] = st * decay[:, sl] + _dot_tn(v[:, sl], k_out[:, sl])
        y_ref[rows, sl] = (_rms_scale(o) * ng_ref[:, sl]).astype(BF16) * gs[:, sl]


def _hgrn_halving_scores(c, consts, pb_ref, q, k, b):
    C = HG_CHUNK
    tril, level_masks, diag_mask, sign4, half_causal_mask = consts
    rows = pl.ds(c * C, C)
    factors = []
    for m in (32, 16, 8):
        parts = []
        for g in range(C // (2 * m)):
            lo = 2 * m * g
            bp = jnp.broadcast_to(b[lo + m - 1:lo + m, :], (m, HG_WIDTH))
            parts += [bp - b[lo:lo + m, :], b[lo + m:lo + 2 * m, :] - bp]
        factors.append(jnp.exp2(jnp.concatenate(parts, axis=0)).astype(BF16))
    bp4 = jnp.concatenate([jnp.broadcast_to(b[8 * g + 3:8 * g + 4, :], (8, HG_WIDTH)) for g in range(C // 8)], axis=0)
    factors.append(jnp.exp2((b - bp4) * sign4).astype(BF16))
    factors.append(pb_ref[rows, C_E2:C_E2 + 512])
    factors.append(pb_ref[rows, C_E1:C_E1 + 512])

    qk = q * k
    att = [jnp.where(diag_mask, jnp.sum(qk[:, h * HG_DK:(h + 1) * HG_DK].astype(F32), axis=-1, keepdims=True), 0.0)
           for h in range(HG_HEADS)]
    for e, mask in zip(factors, level_masks):
        qt = q * e
        kt = k * e
        for h in range(HG_HEADS):
            sl = slice(h * HG_DK, (h + 1) * HG_DK)
            att[h] = jnp.where(mask, _dot_nt(qt[:, sl], kt[:, sl]), att[h])
    return att


def _swa_consts():
    W = WINDOW
    qi = lax.broadcasted_iota(jnp.int32, (W, 4 * W), 0)
    kj = lax.broadcasted_iota(jnp.int32, (W, 4 * W), 1) & (2 * W - 1)
    rel = kj - qi
    band = (rel >= 1) & (rel <= W)
    lane = lax.broadcasted_iota(jnp.int32, (2 * W, W), 1)
    ones_rows = jnp.concatenate([jnp.where(lane < AT_HEAD_DIM, 1.0, 0.0),
                                 jnp.where(lane >= AT_HEAD_DIM, 1.0, 0.0)], axis=0).astype(BF16)
    return band, kj, ones_rows


def _swa_block(j, first_block, consts, sink_ref, pb_ref, kd_ref, vd_ref, y_ref):
    W = WINDOW
    D = AT_HEAD_DIM
    NP = AT_GROUP // 2
    band, kj, ones_rows = consts
    r0 = j * W
    valid = band if first_block is None else band & (kj >= jnp.where(first_block, W, 0))
    valid = jnp.concatenate([valid] * NP, axis=0)
    lane = lax.broadcasted_iota(jnp.int32, (NP * W, 2 * D), 1)
    upper = lax.broadcasted_iota(jnp.int32, (NP * W, 1), 0) >= W
    for hk in range(AT_KV_HEADS):
        kd = kd_ref[pl.ds(r0, 2 * W), hk * 2 * D:(hk + 1) * 2 * D]
        vd = vd_ref[pl.ds(r0, 2 * W), hk * 2 * D:(hk + 1) * 2 * D]
        k_blocks = _row_blocks(kd, D)
        v_blocks = jnp.concatenate([_row_blocks(vd, D), ones_rows], axis=1)
        c0 = hk * AT_GROUP * D
        q = jnp.concatenate([pb_ref[pl.ds(r0, W), C_AQ + c0 + 2 * D * pr:C_AQ + c0 + 2 * D * (pr + 1)]
                             for pr in range(NP)], axis=0)
        s = jnp.where(valid, _dot_nt(q, k_blocks), MASK_VALUE)
        sk = [jnp.where(upper, sink_ref[hk * AT_GROUP + 2 + i], sink_ref[hk * AT_GROUP + i]) * LOG2E
              for i in range(2)]
        m = [jnp.maximum(jnp.max(s[:, i * 2 * W:(i + 1) * 2 * W], axis=-1, keepdims=True), sk[i]) for i in range(2)]
        e = jnp.concatenate([jnp.exp2(s[:, i * 2 * W:(i + 1) * 2 * W] - m[i]) for i in range(2)], axis=1)
        o = _dot(e.astype(BF16), v_blocks)
        sink_term = jnp.where(lane < D, jnp.exp2(sk[0] - m[0]), jnp.exp2(sk[1] - m[1]))
        out = (o[:, 0:2 * D] / (o[:, 2 * D:4 * D] + sink_term)).astype(BF16)
        for pr in range(NP):
            y_ref[pl.ds(r0, W), c0 + 2 * D * pr:c0 + 2 * D * (pr + 1)] = out[pr * W:(pr + 1) * W, :]


def _dup_heads(x):
    lane = lax.broadcasted_iota(jnp.int32, x.shape, 1)
    swapped = pltpu.roll(x, AT_HEAD_DIM, 1)
    low = lane < AT_HEAD_DIM
    return jnp.concatenate([jnp.where(low, x, swapped), jnp.where(low, swapped, x)], axis=1)


def _mixer_kernel(sink_ref, pb_ref, kvprev_ref, h_ref, wout_ref, ng_ref, o_ref,
                  st_ref, yh_ref, ya_ref, kd_ref, vd_ref, b_ref, *, ts):
    si = pl.program_id(1)
    W = WINDOW

    @pl.when(si == 0)
    def _():
        st_ref[...] = jnp.zeros_like(st_ref)

    hgrn_consts = _hgrn_consts()
    tril = hgrn_consts[0]
    for c in range(ts // HG_CHUNK):
        rows = slice(c * HG_CHUNK, (c + 1) * HG_CHUNK)
        bw = _dot(tril, pb_ref[rows, C_LH:C_LH + 1024])
        b_ref[rows, :] = bw[:, 0:512] + bw[:, 512:1024]

    kd_ref[0:W, :] = _dup_heads(kvprev_ref[:, 0:KV_WIDTH].astype(F32)).astype(BF16)
    vd_ref[0:W, :] = _dup_heads(kvprev_ref[:, KV_WIDTH:2 * KV_WIDTH].astype(F32)).astype(BF16)
    kd_ref[W:W + ts, :] = _dup_heads(pb_ref[:, C_AK:C_AK + KV_WIDTH].astype(F32)).astype(BF16)
    vd_ref[W:W + ts, :] = _dup_heads(pb_ref[:, C_AK + KV_WIDTH:C_AK + 2 * KV_WIDTH].astype(F32)).astype(BF16)
    swa_consts = _swa_consts()

    for j in range(ts // W):
        first_block = (si == 0) if j == 0 else None
        _swa_block(j, first_block, swa_consts, sink_ref, pb_ref, kd_ref, vd_ref, ya_ref)
    acc = h_ref[...] + _dot(ya_ref[...], wout_ref[HG_WIDTH:D_MODEL, :])

    H = HG_CHUNK // 2
    half_decays = []
    for c in range(ts // HG_CHUNK):
        mid = b_ref[c * HG_CHUNK + H - 1:c * HG_CHUNK + H, :]
        last = b_ref[(c + 1) * HG_CHUNK - 1:(c + 1) * HG_CHUNK, :]
        half_decays += [mid, last - mid]
    bounded = jnp.min(functools.reduce(jnp.minimum, half_decays)) >= -HG_SINGLE_PIVOT_MAX_LOG2_DECAY
    for single_pivot in (True, False):
        @pl.when(bounded == single_pivot)
        def _():
            for c in range(ts // HG_CHUNK):
                _hgrn_chunk(c, hgrn_consts, pb_ref, b_ref, ng_ref, st_ref, yh_ref, single_pivot)

    o_ref[...] = acc + _dot(yh_ref[...], wout_ref[0:HG_WIDTH, :])


def _mixer(pb, h, w_out, ng, sinks, ts, layer):
    B, S, _ = h.shape
    nblk = ts // WINDOW
    const = lambda b, s: (0, 0)
    return pl.pallas_call(
        functools.partial(_mixer_kernel, ts=ts),
        grid=(B, S // ts),
        in_specs=[
            pl.BlockSpec(memory_space=pltpu.SMEM),
            pl.BlockSpec((None, ts, PB_WIDTH), lambda b, s: (b, s, 0)),
            pl.BlockSpec((None, WINDOW, 2 * KV_WIDTH),
                         lambda b, s: (b, jnp.maximum(s * nblk - 1, 0), C_AK // (2 * KV_WIDTH))),
            pl.BlockSpec((None, ts, D_MODEL), lambda b, s: (b, s, 0)),
            _layer_weight(layer, D_MODEL, D_MODEL),
            pl.BlockSpec((1, HG_WIDTH), const),
        ],
        out_specs=pl.BlockSpec((None, ts, D_MODEL), lambda b, s: (b, s, 0)),
        out_shape=jax.ShapeDtypeStruct((B, S, D_MODEL), F32),
        scratch_shapes=[
            pltpu.VMEM((HG_HEADS, HG_DK, HG_DK), F32),
            pltpu.VMEM((ts, HG_WIDTH), BF16),
            pltpu.VMEM((ts, AT_WIDTH), BF16),
            pltpu.VMEM((ts + WINDOW, 2 * KV_WIDTH), BF16),
            pltpu.VMEM((ts + WINDOW, 2 * KV_WIDTH), BF16),
            pltpu.VMEM((ts, HG_WIDTH), F32),
        ],
        compiler_params=pltpu.CompilerParams(
            dimension_semantics=("arbitrary", "arbitrary"), vmem_limit_bytes=VMEM_LIMIT_BYTES),
        name="mixer",
    )(sinks, pb, pb, h, w_out, ng)


def _ffn_kernel(h_ref, gf_ref, wup_ref, cw_ref, cb_ref, wdn_ref, gp_ref, wpg_ref, p_ref, wpu_ref,
                gfin_ref, o_ref, a_ref, hist_ref, gate_ref, *, ts, final_norm):
    si = pl.program_id(1)
    P = CONV_PAD

    @pl.when(si == 0)
    def _():
        hist_ref[...] = jnp.zeros_like(hist_ref)

    h = h_ref[...]
    u = (_rms_scale(h) * gf_ref[...]).astype(BF16)
    for c in range(N_FF_CHUNKS):
        cs = slice(c * FF_CHUNK, (c + 1) * FF_CHUNK)
        a = _dot(u, wup_ref[:, cs])
        b = _dot(u, wup_ref[:, D_FF + c * FF_CHUNK:D_FF + (c + 1) * FF_CHUNK])
        w = c % 2
        a_ref[w, 0:P, :] = hist_ref[c]
        a_ref[w, P:P + ts, :] = a
        conv = (a * cw_ref[2:3, cs] + a_ref[w, P - 1:P - 1 + ts, :] * cw_ref[1:2, cs]
                + a_ref[w, P - 2:P - 2 + ts, :] * cw_ref[0:1, cs] + cb_ref[:, cs])
        gate_ref[:, cs] = (conv * _sigmoid(conv) * b).astype(BF16)
        hist_ref[c] = a_ref[w, ts:ts + P, :]
    h1 = h + _dot(gate_ref[...], wdn_ref[...])

    u2 = (_rms_scale(h1) * gp_ref[...]).astype(BF16)
    ple_gate = _sigmoid(_dot(u2, wpg_ref[...]))
    h2 = h1 + ple_gate * _dot(p_ref[...].astype(BF16), wpu_ref[...])
    if final_norm:
        h2 = _rms_scale(h2) * gfin_ref[...]
    o_ref[...] = h2


def _ffn(h, gf, w_up, cw, cb, w_dn, gp, wpg, p, wpu, gfin, ts, layer, final_norm):
    B, S, _ = h.shape
    const = lambda b, s: (0, 0)
    resident = functools.partial(_layer_weight, layer)
    return pl.pallas_call(
        functools.partial(_ffn_kernel, ts=ts, final_norm=final_norm),
        grid=(B, S // ts),
        in_specs=[
            pl.BlockSpec((None, ts, D_MODEL), lambda b, s: (b, s, 0)),
            pl.BlockSpec((1, D_MODEL), const),
            resident(D_MODEL, 2 * D_FF),
            pl.BlockSpec((3, D_FF), const),
            pl.BlockSpec((1, D_FF), const),
            resident(D_FF, D_MODEL),
            pl.BlockSpec((1, D_MODEL), const),
            resident(D_MODEL, D_MODEL),
            pl.BlockSpec((None, None, ts, PLE_DIM), lambda b, s: (layer, b, s, 0)),
            resident(PLE_DIM, D_MODEL),
            pl.BlockSpec((1, D_MODEL), const),
        ],
        out_specs=pl.BlockSpec((None, ts, D_MODEL), lambda b, s: (b, s, 0)),
        out_shape=jax.ShapeDtypeStruct((B, S, D_MODEL), F32),
        scratch_shapes=[
            pltpu.VMEM((2, CONV_PAD + ts, FF_CHUNK), F32),
            pltpu.VMEM((N_FF_CHUNKS, CONV_PAD, FF_CHUNK), F32),
            pltpu.VMEM((ts, D_FF), BF16),
        ],
        compiler_params=pltpu.CompilerParams(
            dimension_semantics=("arbitrary", "arbitrary"), vmem_limit_bytes=VMEM_LIMIT_BYTES),
        name="ffn_ple",
    )(h, gf, w_up, cw, cb, w_dn, gp, wpg, p, wpu, gfin)


def kernel(x, p, g_mix, w_in, lb_logits, hg_norm_g, attn_sinks, w_out, g_ffn, w_up, conv_w, conv_b,
           w_down, g_ple, w_ple_gate, w_ple_up, g_final):
    depth = w_in.shape[0]
    B, S, _ = x.shape
    ts_in, ts_mix, ts = (min(t, S) for t in (INPROJ_TILE, MIXER_TILE, TOKEN_TILE))
    assert all(S % t == 0 and t % WINDOW == 0 for t in (ts_in, ts_mix, ts))

    lb_p = jax.nn.softmax(lb_logits.astype(F32), axis=0)
    lb_all = jnp.cumsum(lb_p, axis=0) - lb_p[0]
    lb_floor = jnp.maximum(lb_all, LB_FLOOR)
    lbc = jnp.stack([jnp.log(lb_floor) * LOG2E, jnp.log1p(-lb_all) * LOG2E, 1.0 - lb_all, lb_floor], axis=1)

    w_in, w_out, w_up, w_down, w_ple_gate, w_ple_up = (
        w.astype(BF16) for w in (w_in, w_out, w_up, w_down, w_ple_gate, w_ple_up))
    row = lambda a: a.reshape(1, -1).astype(F32)
    h = x
    for i in range(depth):
        pb = _inproj(h, row(g_mix[i]), w_in, lbc[i], ts_in, i)
        h = _mixer(pb, h, w_out, row(hg_norm_g[i]), attn_sinks[i].astype(F32), ts_mix, i)
        h = _ffn(h, row(g_ffn[i]), w_up, conv_w[i].astype(F32), row(conv_b[i]), w_down, row(g_ple[i]),
                 w_ple_gate, p, w_ple_up, row(g_final), ts, i, final_norm=(i == depth - 1))
    return h
```
